```python
import math
import jax, jax.numpy as jnp
from jax import lax
import numpy as np

D_MODEL = 1024
BATCH = 8
SEQ = 4096
DEPTH = 1

MLA_HEADS = 8
MLA_NOPE = 64
MLA_ROPE = 32
MLA_V = 64
MLA_Q_RANK = 384
MLA_KV_RANK = 256
RET_HEADS = 4
RET_DK = 64
RET_DV = 128
RET_CHUNK = 128
FF_DIM = 2816
Q_BLOCK = 128
ROPE_BASE = 10000.0
NORM_EPS = 1e-6
GN_EPS = 1e-6

IN_SIZES = (MLA_Q_RANK, MLA_KV_RANK, MLA_ROPE,
            RET_HEADS * RET_DK, RET_HEADS * RET_DK, RET_HEADS * RET_DV, RET_HEADS * RET_DV,
            D_MODEL, D_MODEL)
IN_WIDTH = sum(IN_SIZES)

kernel_name = "hybrid_mla_retention_gated_macaron"


def _rmsnorm(x, w):
    xf = x.astype(jnp.float32)
    y = xf * lax.rsqrt(jnp.mean(xf * xf, axis=-1, keepdims=True) + NORM_EPS)
    return (y * w.astype(jnp.float32)).astype(x.dtype)


def _swiglu(x, w1, w2):
    gate, up = jnp.split(x @ w1, 2, axis=-1)
    return (jax.nn.silu(gate) * up) @ w2


def _rope(x, positions):
    half = x.shape[-1] // 2
    inv_freq = ROPE_BASE ** (-jnp.arange(half, dtype=jnp.float32) / half)
    ang = positions.astype(jnp.float32)[..., None] * inv_freq
    cos = jnp.cos(ang)[:, :, None, :]
    sin = jnp.sin(ang)[:, :, None, :]
    xf = x.astype(jnp.float32)
    x1, x2 = xf[..., :half], xf[..., half:]
    return jnp.concatenate([x1 * cos - x2 * sin, x2 * cos + x1 * sin], axis=-1).astype(x.dtype)


def _split_points():
    points, acc = [], 0
    for s in IN_SIZES[:-1]:
        acc += s
        points.append(acc)
    return points


def _causal_attention(q, k, v, scale):
    b, h, s, dqk = q.shape
    nb = s // Q_BLOCK
    q_blocks = jnp.moveaxis(q.reshape(b, h, nb, Q_BLOCK, dqk), 2, 0)
    k_pos = jnp.arange(s)

    def one_block(args):
        q_blk, i = args
        scores = jnp.einsum('bhqd,bhkd->bhqk', q_blk, k).astype(jnp.float32) * scale
        q_pos = i * Q_BLOCK + jnp.arange(Q_BLOCK)
        scores = jnp.where(q_pos[:, None] >= k_pos[None, :], scores, -jnp.inf)
        p = jax.nn.softmax(scores, axis=-1)
        return jnp.einsum('bhqk,bhkd->bhqd', p.astype(v.dtype), v)

    out = lax.map(one_block, (q_blocks, jnp.arange(nb)))
    return jnp.moveaxis(out, 0, 2).reshape(b, h, s, v.shape[-1])


def _chunkwise_retention(q, k, v):
    b, h, s, dk = q.shape
    dv = v.shape[-1]
    n = s // RET_CHUNK
    log_gamma = jnp.log(1.0 - 2.0 ** (-5.0 - jnp.arange(h, dtype=jnp.float32)))
    idx = jnp.arange(RET_CHUNK, dtype=jnp.float32)
    diff = idx[:, None] - idx[None, :]
    decay_intra = jnp.where(diff >= 0, jnp.exp(jnp.maximum(diff, 0.0) * log_gamma[:, None, None]), 0.0)
    zeta = jnp.exp((RET_CHUNK - 1 - idx) * log_gamma[:, None])
    xi = jnp.exp((idx + 1.0) * log_gamma[:, None])
    chunk_decay = jnp.exp(RET_CHUNK * log_gamma)

    qc = q.reshape(b, h, n, RET_CHUNK, dk)
    kc = k.reshape(b, h, n, RET_CHUNK, dk)
    vc = v.reshape(b, h, n, RET_CHUNK, dv)

    inner = jnp.einsum('bhncd,bhnsd->bhncs', qc, kc) * decay_intra[None, :, None].astype(q.dtype)
    y_inner = jnp.einsum('bhncs,bhnse->bhnce', inner, vc)

    kv = jnp.einsum('bhnsd,bhnse->bhnde', kc * zeta[None, :, None, :, None].astype(k.dtype), vc)
    decay_c = chunk_decay.astype(kv.dtype)[None, :, None, None]

    def step(state, kv_n):
        return state * decay_c + kv_n, state

    _, r_prev = lax.scan(step, jnp.zeros_like(kv[:, :, 0]), jnp.moveaxis(kv, 2, 0))
    r_prev = jnp.moveaxis(r_prev, 0, 2)
    y_cross = jnp.einsum('bhncd,bhnde->bhnce', qc, r_prev) * xi[None, :, None, :, None].astype(q.dtype)
    y = (y_inner + y_cross).reshape(b, h, s, dv)

    yf = y.astype(jnp.float32)
    mu = jnp.mean(yf, axis=-1, keepdims=True)
    var = jnp.mean(jnp.square(yf - mu), axis=-1, keepdims=True)
    return ((yf - mu) * lax.rsqrt(var + GN_EPS)).astype(v.dtype)


def _token_mixer(h, positions, w_in, mla_q_norm_w, mla_w_uq, mla_kv_norm_w, mla_w_ukv,
                 ret_gn_w, w_branch_mla, w_branch_ret, w_out):
    b, s, _ = h.shape
    proj = h @ w_in
    c_q, c_kv, k_rope_raw, rq, rk, rv, rg, gate_mla, gate_ret = jnp.split(proj, _split_points(), axis=-1)

    q = (_rmsnorm(c_q, mla_q_norm_w) @ mla_w_uq).reshape(b, s, MLA_HEADS, MLA_NOPE + MLA_ROPE)
    q = jnp.concatenate([q[..., :MLA_NOPE], _rope(q[..., MLA_NOPE:], positions)], axis=-1)
    kv = (_rmsnorm(c_kv, mla_kv_norm_w) @ mla_w_ukv).reshape(b, s, MLA_HEADS, MLA_NOPE + MLA_V)
    k_nope, v_mla = kv[..., :MLA_NOPE], kv[..., MLA_NOPE:]
    k_rope = _rope(k_rope_raw[:, :, None, :], positions)
    k = jnp.concatenate([k_nope, jnp.broadcast_to(k_rope, (b, s, MLA_HEADS, MLA_ROPE))], axis=-1)
    o_mla = _causal_attention(q.transpose(0, 2, 1, 3), k.transpose(0, 2, 1, 3),
                              v_mla.transpose(0, 2, 1, 3), 1.0 / math.sqrt(MLA_NOPE + MLA_ROPE))
    o_mla = o_mla.transpose(0, 2, 1, 3).reshape(b, s, MLA_HEADS * MLA_V) @ w_branch_mla

    rq = _rope(rq.reshape(b, s, RET_HEADS, RET_DK), positions).transpose(0, 2, 1, 3)
    rk = (_rope(rk.reshape(b, s, RET_HEADS, RET_DK), positions) * (RET_DK ** -0.5)).transpose(0, 2, 1, 3)
    rv = rv.reshape(b, s, RET_HEADS, RET_DV).transpose(0, 2, 1, 3)
    y_ret = _chunkwise_retention(rq, rk, rv).transpose(0, 2, 1, 3).reshape(b, s, RET_HEADS * RET_DV)
    o_ret = (jax.nn.silu(rg) * (y_ret * ret_gn_w)) @ w_branch_ret

    merged = jax.nn.sigmoid(gate_mla) * o_mla + jax.nn.sigmoid(gate_ret) * o_ret
    return merged @ w_out


def setup_inputs(seed: int = 0) -> dict:
    key = jax.random.key(seed)
    ks = jax.random.split(key, 24)
    f32 = jnp.float32

    def w(k, shape, fan_in):
        return jax.random.normal(k, (DEPTH,) + shape, f32) * (fan_in ** -0.5)

    def gain(k, n):
        return 1.0 + 0.02 * jax.random.normal(k, (DEPTH, n), f32)

    return {
        "x": jax.random.normal(ks[0], (BATCH, SEQ, D_MODEL), f32),
        "positions": jnp.broadcast_to(jnp.arange(SEQ, dtype=jnp.int32), (BATCH, SEQ)),
        "ffn1_pre_w": gain(ks[1], D_MODEL),
        "ffn1_w1": w(ks[2], (D_MODEL, 2 * FF_DIM), D_MODEL),
        "ffn1_w2": w(ks[3], (FF_DIM, D_MODEL), FF_DIM),
        "ffn1_post_w": gain(ks[4], D_MODEL),
        "mix_pre_w": gain(ks[5], D_MODEL),
        "w_in": w(ks[6], (D_MODEL, IN_WIDTH), D_MODEL),
        "mla_q_norm_w": gain(ks[7], MLA_Q_RANK),
        "mla_w_uq": w(ks[8], (MLA_Q_RANK, MLA_HEADS * (MLA_NOPE + MLA_ROPE)), MLA_Q_RANK),
        "mla_kv_norm_w": gain(ks[9], MLA_KV_RANK),
        "mla_w_ukv": w(ks[10], (MLA_KV_RANK, MLA_HEADS * (MLA_NOPE + MLA_V)), MLA_KV_RANK),
        "ret_gn_w": gain(ks[11], RET_HEADS * RET_DV),
        "w_branch_mla": w(ks[12], (MLA_HEADS * MLA_V, D_MODEL), MLA_HEADS * MLA_V),
        "w_branch_ret": w(ks[13], (RET_HEADS * RET_DV, D_MODEL), RET_HEADS * RET_DV),
        "w_out": w(ks[14], (D_MODEL, D_MODEL), D_MODEL),
        "mix_post_w": gain(ks[15], D_MODEL),
        "ffn2_pre_w": gain(ks[16], D_MODEL),
        "ffn2_w1": w(ks[17], (D_MODEL, 2 * FF_DIM), D_MODEL),
        "ffn2_w2": w(ks[18], (FF_DIM, D_MODEL), FF_DIM),
        "ffn2_post_w": gain(ks[19], D_MODEL),
    }


def reference(x, positions, ffn1_pre_w, ffn1_w1, ffn1_w2, ffn1_post_w, mix_pre_w, w_in,
              mla_q_norm_w, mla_w_uq, mla_kv_norm_w, mla_w_ukv, ret_gn_w, w_branch_mla,
              w_branch_ret, w_out, mix_post_w, ffn2_pre_w, ffn2_w1, ffn2_w2, ffn2_post_w):
    h = x
    for l in range(DEPTH):
        f = _swiglu(_rmsnorm(h, ffn1_pre_w[l]), ffn1_w1[l], ffn1_w2[l])
        h = h + 0.5 * _rmsnorm(f, ffn1_post_w[l])
        m = _token_mixer(_rmsnorm(h, mix_pre_w[l]), positions, w_in[l], mla_q_norm_w[l], mla_w_uq[l],
                         mla_kv_norm_w[l], mla_w_ukv[l], ret_gn_w[l], w_branch_mla[l],
                         w_branch_ret[l], w_out[l])
        h = h + _rmsnorm(m, mix_post_w[l])
        f = _swiglu(_rmsnorm(h, ffn2_pre_w[l]), ffn2_w1[l], ffn2_w2[l])
        h = h + 0.5 * _rmsnorm(f, ffn2_post_w[l])
    return h
```

```python
import functools
import math

import jax
import jax.numpy as jnp
from jax import lax
from jax.experimental import pallas as pl
from jax.experimental.pallas import tpu as pltpu

D_MODEL = 1024
MLA_HEADS = 8
MLA_NOPE = 64
MLA_ROPE = 32
MLA_V = 64
MLA_Q_RANK = 384
MLA_KV_RANK = 256
RET_HEADS = 4
RET_DK = 64
RET_DV = 128
RET_CHUNK = 128
FF_DIM = 2816
ROPE_BASE = 10000.0
NORM_EPS = 1e-6
GN_EPS = 1e-6

LANES = 128
HEAD_PAD = 128
RET_QK = RET_HEADS * RET_DK
RET_V = RET_HEADS * RET_DV
VMEM_LIMIT = 56 * 1024 * 1024

BF16 = jnp.bfloat16
F32 = jnp.float32

PRE_CQ = 0
PRE_CKV = PRE_CQ + MLA_Q_RANK
PRE_KR = PRE_CKV + MLA_KV_RANK
PRE_RQ = PRE_KR + HEAD_PAD
PRE_RK = PRE_RQ + RET_QK
PRE_RV = PRE_RK + RET_QK
PRE_WIDTH = PRE_RV + RET_V


def _rms(x, w):
    return x * lax.rsqrt(jnp.mean(x * x, axis=-1, keepdims=True) + NORM_EPS) * w


def _dot(a, b):
    return jnp.dot(a, b, preferred_element_type=F32)


def _dot_nt(a, b):
    return lax.dot_general(a, b, (((1,), (1,)), ((), ())), preferred_element_type=F32)


def _dot_tn(a, b):
    return lax.dot_general(a, b, (((0,), (0,)), ((), ())), preferred_element_type=F32)


def _const_spec(shape):
    nd = len(shape)
    return pl.BlockSpec(shape, lambda *_: (0,) * nd, pipeline_mode=pl.Buffered(1))


def _ffn_chunks():
    sizes, left = [], FF_DIM
    while left > 0:
        c = min(512, left)
        sizes.append(c)
        left -= c
    return tuple(sizes)


def _ffn_kernel(h_ref, pre_ref, w1_ref, w2_ref, post_ref, o_ref):
    h = h_ref[...]
    xn = _rms(h, pre_ref[...]).astype(BF16)
    acc = jnp.zeros(h.shape, F32)
    off = 0
    for c in _ffn_chunks():
        g = _dot(xn, w1_ref[:, off:off + c])
        u = _dot(xn, w1_ref[:, FF_DIM + off:FF_DIM + off + c])
        a = (g * jax.nn.sigmoid(g) * u).astype(BF16)
        acc = acc + _dot(a, w2_ref[off:off + c, :])
        off += c
    o_ref[...] = h + 0.5 * _rms(acc, post_ref[...])


def _ffn(h, pre_w, w1, w2, post_w, tm):
    t = h.shape[0]
    row = pl.BlockSpec((tm, D_MODEL), lambda i: (i, 0))
    return pl.pallas_call(
        _ffn_kernel,
        grid=(t // tm,),
        in_specs=[row, _const_spec((1, D_MODEL)), _const_spec((D_MODEL, 2 * FF_DIM)),
                  _const_spec((FF_DIM, D_MODEL)), _const_spec((1, D_MODEL))],
        out_specs=row,
        out_shape=jax.ShapeDtypeStruct((t, D_MODEL), F32),
        compiler_params=pltpu.CompilerParams(dimension_semantics=("parallel",),
                                             vmem_limit_bytes=VMEM_LIMIT),
        name="ffn",
    )(h, pre_w, w1, w2, post_w)


def _rope_lanes(x, c, sa, sb):
    return x * c + pltpu.roll(x, LANES - MLA_ROPE // 2, 1) * sa + pltpu.roll(x, MLA_ROPE // 2, 1) * sb


def _tile4(x):
    y = x + pltpu.roll(x, 32, 1)
    return y + pltpu.roll(y, 64, 1)


def _pre_kernel(h_ref, pos_ref, rows_ref, pre_ref, win_ref, qn_ref, wuq_ref, kvn_ref, wukv_ref,
                q_ref, k_ref, v_ref, rq_ref, rk_ref, rv_ref):
    hn = _rms(h_ref[...], pre_ref[...]).astype(BF16)
    proj = _dot(hn, win_ref[...])

    rows = rows_ref[...]
    ang = pos_ref[...].astype(F32) * rows[0:1, :]
    cos, sin = jnp.cos(ang), jnp.sin(ang)
    c_q = rows[1:2, :] + cos * rows[2:3, :]
    s_a = sin * rows[3:4, :]
    s_b = sin * rows[4:5, :]
    c_r = _tile4(cos * rows[5:6, :])
    s_r = _tile4(sin * rows[5:6, :])

    cq = _rms(proj[:, PRE_CQ:PRE_CKV], qn_ref[...]).astype(BF16)
    q_raw = _dot(cq, wuq_ref[...])
    ckv = _rms(proj[:, PRE_CKV:PRE_KR], kvn_ref[...]).astype(BF16)
    kv_raw = _dot(ckv, wukv_ref[...])
    kr = _rope_lanes(proj[:, PRE_KR:PRE_RQ], c_q, s_a, s_b)
    for hd in range(MLA_HEADS):
        sl = slice(hd * HEAD_PAD, (hd + 1) * HEAD_PAD)
        q_ref[:, sl] = _rope_lanes(q_raw[:, sl], c_q, s_a, s_b).astype(BF16)
        k_ref[:, sl] = (kv_raw[:, sl] + kr).astype(BF16)
    v_ref[...] = kv_raw[:, MLA_HEADS * HEAD_PAD:].astype(BF16)

    half = RET_QK // 2
    rq1, rq2 = proj[:, PRE_RQ:PRE_RQ + half], proj[:, PRE_RQ + half:PRE_RK]
    rq_ref[:, :half] = (rq1 * c_r - rq2 * s_r).astype(BF16)
    rq_ref[:, half:] = (rq2 * c_r + rq1 * s_r).astype(BF16)
    rk1, rk2 = proj[:, PRE_RK:PRE_RK + half], proj[:, PRE_RK + half:PRE_RV]
    k_scale = RET_DK ** -0.5
    rk_ref[:, :half] = (rk1 * c_r - rk2 * s_r) * k_scale
    rk_ref[:, half:] = (rk2 * c_r + rk1 * s_r) * k_scale
    rv_ref[...] = proj[:, PRE_RV:].astype(BF16)


def _mixer_pre(h, pos, rows, pre_w, w_in_pre, qn_w, w_uq, kvn_w, w_ukv, tm):
    t = h.shape[0]

    def row(width):
        return pl.BlockSpec((tm, width), lambda i: (i, 0))

    qk_w = MLA_HEADS * HEAD_PAD
    v_w = MLA_HEADS * MLA_V
    return pl.pallas_call(
        _pre_kernel,
        grid=(t // tm,),
        in_specs=[row(D_MODEL), row(1), _const_spec((8, LANES)), _const_spec((1, D_MODEL)),
                  _const_spec((D_MODEL, PRE_WIDTH)), _const_spec((1, MLA_Q_RANK)),
                  _const_spec((MLA_Q_RANK, qk_w)), _const_spec((1, MLA_KV_RANK)),
                  _const_spec((MLA_KV_RANK, qk_w + v_w))],
        out_specs=[row(qk_w), row(qk_w), row(v_w), row(RET_QK), row(RET_QK), row(RET_V)],
        out_shape=[jax.ShapeDtypeStruct((t, qk_w), BF16), jax.ShapeDtypeStruct((t, qk_w), BF16),
                   jax.ShapeDtypeStruct((t, v_w), BF16), jax.ShapeDtypeStruct((t, RET_QK), BF16),
                   jax.ShapeDtypeStruct((t, RET_QK), F32), jax.ShapeDtypeStruct((t, RET_V), BF16)],
        compiler_params=pltpu.CompilerParams(dimension_semantics=("parallel",),
                                             vmem_limit_bytes=VMEM_LIMIT),
        name="mixer_pre",
    )(h, pos, rows, pre_w, w_in_pre, qn_w, w_uq, kvn_w, w_ukv)


def _attn_kernel(q_ref, k_ref, v_ref, o_ref, *, tq, scale):
    qi = pl.program_id(2)
    q = [q_ref[:, hd * HEAD_PAD:(hd + 1) * HEAD_PAD] for hd in range(2)]

    def tile(j, carry, masked):
        start = pl.multiple_of(j * tq, tq)
        v = v_ref[pl.ds(start, tq), :]
        out = []
        for hd in range(2):
            m, l, acc = carry[hd]
            k = k_ref[pl.ds(start, tq), hd * HEAD_PAD:(hd + 1) * HEAD_PAD]
            s = _dot_nt(q[hd], k) * scale
            if masked:
                r = lax.broadcasted_iota(jnp.int32, s.shape, 0)
                c = lax.broadcasted_iota(jnp.int32, s.shape, 1)
                s = jnp.where(r >= c, s, -jnp.inf)
            m_new = jnp.maximum(m, jnp.max(s, axis=-1, keepdims=True))
            alpha = jnp.exp(m - m_new)
            p = jnp.exp(s - m_new)
            l = alpha * l + jnp.sum(p, axis=-1, keepdims=True)
            acc = alpha * acc + _dot(p.astype(BF16), v)
            out.append((m_new, l, acc))
        return tuple(out)

    init = tuple((jnp.full((tq, 1), -jnp.inf, F32), jnp.zeros((tq, 1), F32),
                  jnp.zeros((tq, 2 * MLA_V), F32)) for _ in range(2))
    carry = lax.fori_loop(0, qi, lambda j, c: tile(j, c, False), init)
    (_, l0, a0), (_, l1, a1) = tile(qi, carry, True)
    lane = lax.broadcasted_iota(jnp.int32, a0.shape, 1)
    o_ref[...] = jnp.where(lane < MLA_V, a0 / l0, a1 / l1).astype(o_ref.dtype)


def _attention(q, k, v, batch, seq, tq):
    t = q.shape[0]
    nq = seq // tq
    pairs = MLA_HEADS // 2
    scale = 1.0 / math.sqrt(MLA_NOPE + MLA_ROPE)
    return pl.pallas_call(
        functools.partial(_attn_kernel, tq=tq, scale=scale),
        grid=(batch, pairs, nq),
        in_specs=[pl.BlockSpec((tq, 2 * HEAD_PAD), lambda b, p, i: (b * nq + i, p)),
                  pl.BlockSpec((seq, 2 * HEAD_PAD), lambda b, p, i: (b, p)),
                  pl.BlockSpec((seq, 2 * MLA_V), lambda b, p, i: (b, p))],
        out_specs=pl.BlockSpec((tq, 2 * MLA_V), lambda b, p, i: (b * nq + i, p)),
        out_shape=jax.ShapeDtypeStruct((t, MLA_HEADS * MLA_V), BF16),
        compiler_params=pltpu.CompilerParams(
            dimension_semantics=("parallel", "parallel", "arbitrary"),
            vmem_limit_bytes=VMEM_LIMIT),
        name="attention",
    )(q, k, v)


def _ret_kernel(rq_ref, rk_ref, rv_ref, dec_ref, xi_ref, zeta_ref, cd_ref, mask_ref, y_ref, state_ref,
                *, chunks):
    @pl.when(pl.program_id(1) == 0)
    def _():
        state_ref[...] = jnp.zeros(state_ref.shape, F32)

    for c in range(chunks):
        rows = slice(c * RET_CHUNK, (c + 1) * RET_CHUNK)
        q = rq_ref[rows, :]
        kf = rk_ref[rows, :]
        for hd in range(RET_HEADS):
            cols = slice(hd * RET_DV, (hd + 1) * RET_DV)
            v = rv_ref[rows, cols]
            km = kf * mask_ref[hd]
            inner = (_dot_nt(q, km.astype(BF16)) * dec_ref[hd]).astype(BF16)
            state = state_ref[hd]
            y = _dot(inner, v) + _dot(q, state.astype(BF16)) * xi_ref[hd]
            state_ref[hd] = state * cd_ref[hd] + _dot_tn((km * zeta_ref[hd]).astype(BF16), v)
            mu = jnp.mean(y, axis=-1, keepdims=True)
            yc = y - mu
            var = jnp.mean(yc * yc, axis=-1, keepdims=True)
            y_ref[rows, cols] = yc * lax.rsqrt(var + GN_EPS)


def _retention(rq, rk, rv, consts, batch, seq, tm):
    t = rq.shape[0]
    ns = seq // tm
    dec, xi, zeta, cd, mask = consts

    def row(width):
        return pl.BlockSpec((tm, width), lambda b, i: (b * ns + i, 0))

    return pl.pallas_call(
        functools.partial(_ret_kernel, chunks=tm // RET_CHUNK),
        grid=(batch, ns),
        in_specs=[row(RET_QK), row(RET_QK), row(RET_V), _const_spec(dec.shape), _const_spec(xi.shape),
                  _const_spec(zeta.shape), _const_spec(cd.shape), _const_spec(mask.shape)],
        out_specs=row(RET_V),
        out_shape=jax.ShapeDtypeStruct((t, RET_V), F32),
        scratch_shapes=[pltpu.VMEM((RET_HEADS, RET_QK, RET_DV), F32)],
        compiler_params=pltpu.CompilerParams(dimension_semantics=("parallel", "arbitrary"),
                                             vmem_limit_bytes=VMEM_LIMIT),
        name="retention",
    )(rq, rk, rv, dec, xi, zeta, cd, mask)


def _retention_consts():
    hh = jnp.arange(RET_HEADS, dtype=F32)
    log_gamma = jnp.log(1.0 - 2.0 ** (-5.0 - hh))
    idx = jnp.arange(RET_CHUNK, dtype=F32)
    diff = idx[:, None] - idx[None, :]
    dec = jnp.where(diff >= 0, jnp.exp(jnp.maximum(diff, 0.0) * log_gamma[:, None, None]), 0.0)
    zeta = jnp.exp((RET_CHUNK - 1 - idx) * log_gamma[:, None])
    xi = jnp.exp((idx + 1.0) * log_gamma[:, None])
    cd = jnp.exp(RET_CHUNK * log_gamma)
    xi_b = jnp.broadcast_to(xi[:, :, None], (RET_HEADS, RET_CHUNK, RET_DV))
    zeta_b = jnp.broadcast_to(zeta[:, :, None], (RET_HEADS, RET_CHUNK, RET_QK))
    cd_b = jnp.broadcast_to(cd[:, None, None], (RET_HEADS, 1, RET_DV))
    lane = jnp.arange(RET_QK)
    head_of_lane = (lane % (RET_QK // 2)) // (RET_DK // 2)
    mask = (head_of_lane[None, :] == jnp.arange(RET_HEADS)[:, None]).astype(F32)[:, None, :]
    return dec, xi_b, zeta_b, cd_b, mask


def _post_kernel(h_ref, o_ref, y_ref, pre_ref, wg_ref, gn_ref, wbm_ref, wbr_ref, wout_ref, post_ref,
                 out_ref):
    h = h_ref[...]
    hn = _rms(h, pre_ref[...]).astype(BF16)
    g = _dot(hn, wg_ref[...])
    rg = g[:, :RET_V]
    gate_mla = g[:, RET_V:RET_V + D_MODEL]
    gate_ret = g[:, RET_V + D_MODEL:]
    o_mla = _dot(o_ref[...], wbm_ref[...])
    a = (rg * jax.nn.sigmoid(rg)) * (y_ref[...] * gn_ref[...])
    o_ret = _dot(a.astype(BF16), wbr_ref[...])
    merged = jax.nn.sigmoid(gate_mla) * o_mla + jax.nn.sigmoid(gate_ret) * o_ret
    m = _dot(merged.astype(BF16), wout_ref[...])
    out_ref[...] = h + _rms(m, post_ref[...])


def _mixer_post(h, o, y, pre_w, w_gates, gn_w, w_bm, w_br, w_out, post_w, tm):
    t = h.shape[0]

    def row(width):
        return pl.BlockSpec((tm, width), lambda i: (i, 0))

    return pl.pallas_call(
        _post_kernel,
        grid=(t // tm,),
        in_specs=[row(D_MODEL), row(MLA_HEADS * MLA_V), row(RET_V), _const_spec((1, D_MODEL)),
                  _const_spec(w_gates.shape), _const_spec((1, RET_V)), _const_spec(w_bm.shape),
                  _const_spec(w_br.shape), _const_spec(w_out.shape), _const_spec((1, D_MODEL))],
        out_specs=row(D_MODEL),
        out_shape=jax.ShapeDtypeStruct((t, D_MODEL), F32),
        compiler_params=pltpu.CompilerParams(dimension_semantics=("parallel",),
                                             vmem_limit_bytes=VMEM_LIMIT),
        name="mixer_post",
    )(h, o, y, pre_w, w_gates, gn_w, w_bm, w_br, w_out, post_w)


def _rope_rows():
    lane = jnp.arange(LANES)
    half_m, half_r = MLA_ROPE // 2, RET_DK // 2
    f_mla = ROPE_BASE ** (-jnp.arange(half_m, dtype=F32) / half_m)
    f_ret = ROPE_BASE ** (-jnp.arange(half_r, dtype=F32) / half_r)
    freq = jnp.zeros((LANES,), F32)
    freq = freq.at[:half_r].set(f_ret)
    freq = freq.at[MLA_NOPE:MLA_NOPE + half_m].set(f_mla)
    freq = freq.at[MLA_NOPE + half_m:MLA_NOPE + 2 * half_m].set(f_mla)
    x1 = (lane >= MLA_NOPE) & (lane < MLA_NOPE + half_m)
    x2 = (lane >= MLA_NOPE + half_m) & (lane < MLA_NOPE + 2 * half_m)
    rows = [freq, (~(x1 | x2)).astype(F32), (x1 | x2).astype(F32), -x1.astype(F32), x2.astype(F32),
            (lane < half_r).astype(F32), jnp.zeros((LANES,), F32), jnp.zeros((LANES,), F32)]
    return jnp.stack(rows)


def _split_halves(w, heads, dim):
    w = w.reshape(w.shape[0], heads, 2, dim // 2)
    return jnp.concatenate([w[:, :, 0, :].reshape(w.shape[0], -1), w[:, :, 1, :].reshape(w.shape[0], -1)],
                           axis=1)


def _layout_w_in(w_in):
    sizes = (MLA_Q_RANK, MLA_KV_RANK, MLA_ROPE, RET_QK, RET_QK, RET_V, RET_V, D_MODEL, D_MODEL)
    parts, off = [], 0
    for s in sizes:
        parts.append(w_in[:, off:off + s])
        off += s
    w_cq, w_ckv, w_kr, w_rq, w_rk, w_rv, w_rg, w_gm, w_gr = parts
    rows = w_in.shape[0]
    w_kr = jnp.concatenate([jnp.zeros((rows, MLA_NOPE), w_in.dtype), w_kr,
                            jnp.zeros((rows, HEAD_PAD - MLA_NOPE - MLA_ROPE), w_in.dtype)], axis=1)
    w_pre = jnp.concatenate([w_cq, w_ckv, w_kr, _split_halves(w_rq, RET_HEADS, RET_DK),
                             _split_halves(w_rk, RET_HEADS, RET_DK), w_rv], axis=1)
    w_gates = jnp.concatenate([w_rg, w_gm, w_gr], axis=1)
    return w_pre.astype(BF16), w_gates.astype(BF16)


def _layout_w_uq(w_uq):
    w = w_uq.reshape(MLA_Q_RANK, MLA_HEADS, MLA_NOPE + MLA_ROPE)
    w = jnp.pad(w, ((0, 0), (0, 0), (0, HEAD_PAD - MLA_NOPE - MLA_ROPE)))
    return w.reshape(MLA_Q_RANK, MLA_HEADS * HEAD_PAD).astype(BF16)


def _layout_w_ukv(w_ukv):
    w = w_ukv.reshape(MLA_KV_RANK, MLA_HEADS, MLA_NOPE + MLA_V)
    w_k = jnp.pad(w[:, :, :MLA_NOPE], ((0, 0), (0, 0), (0, HEAD_PAD - MLA_NOPE)))
    w_v = w[:, :, MLA_NOPE:]
    return jnp.concatenate([w_k.reshape(MLA_KV_RANK, -1), w_v.reshape(MLA_KV_RANK, -1)], axis=1).astype(BF16)


def _row(w):
    return w.reshape(1, -1)


def kernel(x, positions, ffn1_pre_w, ffn1_w1, ffn1_w2, ffn1_post_w, mix_pre_w, w_in, mla_q_norm_w,
           mla_w_uq, mla_kv_norm_w, mla_w_ukv, ret_gn_w, w_branch_mla, w_branch_ret, w_out, mix_post_w,
           ffn2_pre_w, ffn2_w1, ffn2_w2, ffn2_post_w):
    batch, seq, _ = x.shape
    depth = ffn1_w1.shape[0]
    t = batch * seq
    tm = min(512, seq)
    tq = min(512, seq)
    h = x.reshape(t, D_MODEL)
    pos = positions.reshape(t, 1)
    rows = _rope_rows()
    ret_consts = _retention_consts()
    for l in range(depth):
        h = _ffn(h, _row(ffn1_pre_w[l]), ffn1_w1[l].astype(BF16), ffn1_w2[l].astype(BF16),
                 _row(ffn1_post_w[l]), tm)
        w_pre, w_gates = _layout_w_in(w_in[l])
        q, k, v, rq, rk, rv = _mixer_pre(h, pos, rows, _row(mix_pre_w[l]), w_pre, _row(mla_q_norm_w[l]),
                                         _layout_w_uq(mla_w_uq[l]), _row(mla_kv_norm_w[l]),
                                         _layout_w_ukv(mla_w_ukv[l]), tm)
        o = _attention(q, k, v, batch, seq, tq)
        y = _retention(rq, rk, rv, ret_consts, batch, seq, tm)
        h = _mixer_post(h, o, y, _row(mix_pre_w[l]), w_gates, _row(ret_gn_w[l]),
                        w_branch_mla[l].astype(BF16), w_branch_ret[l].astype(BF16), w_out[l].astype(BF16),
                        _row(mix_post_w[l]), tm)
        h = _ffn(h, _row(ffn2_pre_w[l]), ffn2_w1[l].astype(BF16), ffn2_w2[l].astype(BF16),
                 _row(ffn2_post_w[l]), tm)
    return h.reshape(batch, seq, D_MODEL)
```

```python
import functools
import math

import jax
import jax.numpy as jnp
from jax import lax
from jax.experimental import pallas as pl
from jax.experimental.pallas import tpu as pltpu

D_MODEL = 1024
MLA_HEADS = 8
MLA_NOPE = 64
MLA_ROPE = 32
MLA_V = 64
MLA_Q_RANK = 384
MLA_KV_RANK = 256
RET_HEADS = 4
RET_DK = 64
RET_DV = 128
RET_CHUNK = 128
FF_DIM = 2816
ROPE_BASE = 10000.0
NORM_EPS = 1e-6
GN_EPS = 1e-6

LANES = 128
HEAD_PAD = 128
RET_QK = RET_HEADS * RET_DK
RET_V = RET_HEADS * RET_DV
VMEM_LIMIT = 56 * 1024 * 1024

BF16 = jnp.bfloat16
F32 = jnp.float32

PRE_CQ = 0
PRE_CKV = PRE_CQ + MLA_Q_RANK
PRE_KR = PRE_CKV + MLA_KV_RANK
PRE_RQ = PRE_KR + HEAD_PAD
PRE_RK = PRE_RQ + RET_QK
PRE_RV = PRE_RK + RET_QK
PRE_WIDTH = PRE_RV + RET_V


def _rms(x, w):
    return x * lax.rsqrt(jnp.mean(x * x, axis=-1, keepdims=True) + NORM_EPS) * w


def _dot(a, b):
    return jnp.dot(a, b, preferred_element_type=F32)


def _dot_nt(a, b):
    return lax.dot_general(a, b, (((1,), (1,)), ((), ())), preferred_element_type=F32)


def _dot_tn(a, b):
    return lax.dot_general(a, b, (((0,), (0,)), ((), ())), preferred_element_type=F32)


def _const_spec(shape):
    nd = len(shape)
    return pl.BlockSpec(shape, lambda *_: (0,) * nd, pipeline_mode=pl.Buffered(1))


def _ffn_chunks():
    sizes, left = [], FF_DIM
    while left > 0:
        c = min(512, left)
        sizes.append(c)
        left -= c
    return tuple(sizes)


def _ffn_kernel(h_ref, pre_ref, w1_ref, w2_ref, post_ref, o_ref):
    h = h_ref[...]
    xn = _rms(h, pre_ref[...]).astype(BF16)
    acc = jnp.zeros(h.shape, F32)
    off = 0
    for c in _ffn_chunks():
        g = _dot(xn, w1_ref[:, off:off + c])
        u = _dot(xn, w1_ref[:, FF_DIM + off:FF_DIM + off + c])
        a = (g * jax.nn.sigmoid(g) * u).astype(BF16)
        acc = acc + _dot(a, w2_ref[off:off + c, :])
        off += c
    o_ref[...] = h + 0.5 * _rms(acc, post_ref[...])


def _ffn(h, pre_w, w1, w2, post_w, tm):
    t = h.shape[0]
    row = pl.BlockSpec((tm, D_MODEL), lambda i: (i, 0))
    return pl.pallas_call(
        _ffn_kernel,
        grid=(t // tm,),
        in_specs=[row, _const_spec((1, D_MODEL)), _const_spec((D_MODEL, 2 * FF_DIM)),
                  _const_spec((FF_DIM, D_MODEL)), _const_spec((1, D_MODEL))],
        out_specs=row,
        out_shape=jax.ShapeDtypeStruct((t, D_MODEL), F32),
        compiler_params=pltpu.CompilerParams(dimension_semantics=("parallel",),
                                             vmem_limit_bytes=VMEM_LIMIT),
        name="ffn",
    )(h, pre_w, w1, w2, post_w)


def _rope_lanes(x, c, sa, sb):
    return x * c + pltpu.roll(x, LANES - MLA_ROPE // 2, 1) * sa + pltpu.roll(x, MLA_ROPE // 2, 1) * sb


def _tile4(x):
    y = x + pltpu.roll(x, 32, 1)
    return y + pltpu.roll(y, 64, 1)


def _pre_kernel(h_ref, pos_ref, rows_ref, pre_ref, win_ref, qn_ref, wuq_ref, kvn_ref, wuk_ref, wuvt_ref,
                q_ref, k_ref, vt_ref, rq_ref, rk_ref, rv_ref, *, tk):
    hn = _rms(h_ref[...], pre_ref[...]).astype(BF16)
    proj = _dot(hn, win_ref[...])

    rows = rows_ref[...]
    ang = pos_ref[...].astype(F32) * rows[0:1, :]
    cos, sin = jnp.cos(ang), jnp.sin(ang)
    c_q = rows[1:2, :] + cos * rows[2:3, :]
    s_a = sin * rows[3:4, :]
    s_b = sin * rows[4:5, :]
    c_r = _tile4(cos * rows[5:6, :])
    s_r = _tile4(sin * rows[5:6, :])

    cq = _rms(proj[:, PRE_CQ:PRE_CKV], qn_ref[...]).astype(BF16)
    q_raw = _dot(cq, wuq_ref[...])
    ckv = _rms(proj[:, PRE_CKV:PRE_KR], kvn_ref[...]).astype(BF16)
    k_raw = _dot(ckv, wuk_ref[...])
    kr = _rope_lanes(proj[:, PRE_KR:PRE_RQ], c_q, s_a, s_b)
    for hd in range(MLA_HEADS):
        sl = slice(hd * HEAD_PAD, (hd + 1) * HEAD_PAD)
        q_ref[:, sl] = _rope_lanes(q_raw[:, sl], c_q, s_a, s_b).astype(BF16)
        k_ref[:, sl] = (k_raw[:, sl] + kr).astype(BF16)
    vt = _dot_nt(wuvt_ref[...], ckv)
    row = lax.broadcasted_iota(jnp.int32, vt.shape, 0)
    vt = jnp.where(row % HEAD_PAD == MLA_V, 1.0, vt).astype(BF16)
    for hd in range(MLA_HEADS):
        for j in range(vt.shape[1] // tk):
            vt_ref[0, hd, j] = vt[hd * HEAD_PAD:(hd + 1) * HEAD_PAD, j * tk:(j + 1) * tk]

    half = RET_QK // 2
    rq1, rq2 = proj[:, PRE_RQ:PRE_RQ + half], proj[:, PRE_RQ + half:PRE_RK]
    rq_ref[:, :half] = (rq1 * c_r - rq2 * s_r).astype(BF16)
    rq_ref[:, half:] = (rq2 * c_r + rq1 * s_r).astype(BF16)
    rk1, rk2 = proj[:, PRE_RK:PRE_RK + half], proj[:, PRE_RK + half:PRE_RV]
    k_scale = RET_DK ** -0.5
    rk_ref[:, :half] = (rk1 * c_r - rk2 * s_r) * k_scale
    rk_ref[:, half:] = (rk2 * c_r + rk1 * s_r) * k_scale
    rv_ref[...] = proj[:, PRE_RV:].astype(BF16)


def _mixer_pre(h, pos, rows, pre_w, w_in_pre, qn_w, w_uq, kvn_w, w_uk, w_uvt, batch, seq, tm, tk):
    t = h.shape[0]
    ns = seq // tm

    def row(width):
        return pl.BlockSpec((tm, width), lambda i: (i, 0))

    qk_w = MLA_HEADS * HEAD_PAD
    vt_shape = (batch, MLA_HEADS, seq // tk, HEAD_PAD, tk)
    vt_spec = pl.BlockSpec((1, MLA_HEADS, tm // tk, HEAD_PAD, tk), lambda i: (i // ns, 0, i % ns, 0, 0))
    return pl.pallas_call(
        functools.partial(_pre_kernel, tk=tk),
        grid=(t // tm,),
        in_specs=[row(D_MODEL), row(1), _const_spec((8, LANES)), _const_spec((1, D_MODEL)),
                  _const_spec((D_MODEL, PRE_WIDTH)), _const_spec((1, MLA_Q_RANK)),
                  _const_spec((MLA_Q_RANK, qk_w)), _const_spec((1, MLA_KV_RANK)),
                  _const_spec((MLA_KV_RANK, qk_w)), _const_spec((qk_w, MLA_KV_RANK))],
        out_specs=[row(qk_w), row(qk_w), vt_spec, row(RET_QK), row(RET_QK), row(RET_V)],
        out_shape=[jax.ShapeDtypeStruct((t, qk_w), BF16), jax.ShapeDtypeStruct((t, qk_w), BF16),
                   jax.ShapeDtypeStruct(vt_shape, BF16), jax.ShapeDtypeStruct((t, RET_QK), BF16),
                   jax.ShapeDtypeStruct((t, RET_QK), F32), jax.ShapeDtypeStruct((t, RET_V), BF16)],
        compiler_params=pltpu.CompilerParams(dimension_semantics=("parallel",),
                                             vmem_limit_bytes=VMEM_LIMIT),
        name="mixer_pre",
    )(h, pos, rows, pre_w, w_in_pre, qn_w, w_uq, kvn_w, w_uk, w_uvt)


def _attn_kernel(q_ref, k_ref, vt_ref, o_ref, sa_ref, sb_ref, *, tq, tk, c):
    qi = pl.program_id(2)
    assert tq == 2 * tk
    q = [q_ref[:, hd * HEAD_PAD:(hd + 1) * HEAD_PAD] for hd in range(2)]

    def qk(j, s_ref):
        start = pl.multiple_of(j * tk, tk)
        for hd in range(2):
            s_ref[hd] = _dot_nt(k_ref[pl.ds(start, tk), hd * HEAD_PAD:(hd + 1) * HEAD_PAD], q[hd])

    def softmax_pv(j, s_ref, stats, diag):
        out = []
        for hd in range(2):
            m, acc = stats[hd]
            if diag is None:
                keep = None
            else:
                key = lax.broadcasted_iota(jnp.int32, (tk, tq), 0) + diag * tk
                keep = key <= lax.broadcasted_iota(jnp.int32, (tk, tq), 1)

            def scores():
                s = s_ref[hd]
                return s if keep is None else jnp.where(keep, s, -jnp.inf)

            m_new = jnp.maximum(m, jnp.max(scores(), axis=0, keepdims=True) * c)
            alpha = jnp.exp2(m - m_new)
            p = jnp.exp2(scores() * c - m_new).astype(BF16)
            acc = alpha * acc + _dot(vt_ref[hd, j], p)
            out.append((m_new, acc))
        return tuple(out)

    def step(jj, stats):
        j = 2 * jj
        qk(j + 1, sb_ref)
        stats = softmax_pv(j, sa_ref, stats, None)
        qk(j + 2, sa_ref)
        return softmax_pv(j + 1, sb_ref, stats, None)

    stats = tuple((jnp.full((1, tq), -jnp.inf, F32), jnp.zeros((HEAD_PAD, tq), F32)) for _ in range(2))
    qk(0, sa_ref)
    stats = lax.fori_loop(0, qi, step, stats)
    qk(2 * qi + 1, sb_ref)
    stats = softmax_pv(2 * qi, sa_ref, stats, 0)
    stats = softmax_pv(2 * qi + 1, sb_ref, stats, 1)
    o_t = jnp.concatenate([acc[:MLA_V] / acc[MLA_V:MLA_V + 1] for _, acc in stats], axis=0)
    o_ref[...] = o_t.T.astype(o_ref.dtype)


def _attention(q, k, vt, batch, seq, tq, tk):
    t = q.shape[0]
    nq = seq // tq
    pairs = MLA_HEADS // 2
    c = math.log2(math.e) / math.sqrt(MLA_NOPE + MLA_ROPE)
    return pl.pallas_call(
        functools.partial(_attn_kernel, tq=tq, tk=tk, c=c),
        grid=(batch, pairs, nq),
        in_specs=[pl.BlockSpec((tq, 2 * HEAD_PAD), lambda b, p, i: (b * nq + i, p)),
                  pl.BlockSpec((seq, 2 * HEAD_PAD), lambda b, p, i: (b, p)),
                  pl.BlockSpec((None, 2, seq // tk, HEAD_PAD, tk), lambda b, p, i: (b, p, 0, 0, 0))],
        out_specs=pl.BlockSpec((tq, 2 * MLA_V), lambda b, p, i: (b * nq + i, p)),
        out_shape=jax.ShapeDtypeStruct((t, MLA_HEADS * MLA_V), BF16),
        compiler_params=pltpu.CompilerParams(
            dimension_semantics=("parallel", "parallel", "arbitrary"),
            vmem_limit_bytes=VMEM_LIMIT),
        scratch_shapes=[pltpu.VMEM((2, tk, tq), F32), pltpu.VMEM((2, tk, tq), F32)],
        name="attention",
    )(q, k, vt)


def _ret_kernel(rq_ref, rk_ref, rv_ref, dec_ref, xi_ref, zeta_ref, cd_ref, mask_ref, y_ref, state_ref,
                *, chunks):
    @pl.when(pl.program_id(1) == 0)
    def _():
        state_ref[...] = jnp.zeros(state_ref.shape, F32)

    for c in range(chunks):
        rows = slice(c * RET_CHUNK, (c + 1) * RET_CHUNK)
        q = rq_ref[rows, :]
        kf = rk_ref[rows, :]
        for hd in range(RET_HEADS):
            cols = slice(hd * RET_DV, (hd + 1) * RET_DV)
            v = rv_ref[rows, cols]
            km = kf * mask_ref[hd]
            inner = (_dot_nt(q, km.astype(BF16)) * dec_ref[hd]).astype(BF16)
            state = state_ref[hd]
            y = _dot(inner, v) + _dot(q, state.astype(BF16)) * xi_ref[hd]
            state_ref[hd] = state * cd_ref[hd] + _dot_tn((km * zeta_ref[hd]).astype(BF16), v)
            mu = jnp.mean(y, axis=-1, keepdims=True)
            yc = y - mu
            var = jnp.mean(yc * yc, axis=-1, keepdims=True)
            y_ref[rows, cols] = yc * lax.rsqrt(var + GN_EPS)


def _retention(rq, rk, rv, consts, batch, seq, tm):
    t = rq.shape[0]
    ns = seq // tm
    dec, xi, zeta, cd, mask = consts

    def row(width):
        return pl.BlockSpec((tm, width), lambda b, i: (b * ns + i, 0))

    return pl.pallas_call(
        functools.partial(_ret_kernel, chunks=tm // RET_CHUNK),
        grid=(batch, ns),
        in_specs=[row(RET_QK), row(RET_QK), row(RET_V), _const_spec(dec.shape), _const_spec(xi.shape),
                  _const_spec(zeta.shape), _const_spec(cd.shape), _const_spec(mask.shape)],
        out_specs=row(RET_V),
        out_shape=jax.ShapeDtypeStruct((t, RET_V), F32),
        scratch_shapes=[pltpu.VMEM((RET_HEADS, RET_QK, RET_DV), F32)],
        compiler_params=pltpu.CompilerParams(dimension_semantics=("parallel", "arbitrary"),
                                             vmem_limit_bytes=VMEM_LIMIT),
        name="retention",
    )(rq, rk, rv, dec, xi, zeta, cd, mask)


def _retention_consts():
    hh = jnp.arange(RET_HEADS, dtype=F32)
    log_gamma = jnp.log(1.0 - 2.0 ** (-5.0 - hh))
    idx = jnp.arange(RET_CHUNK, dtype=F32)
    diff = idx[:, None] - idx[None, :]
    dec = jnp.where(diff >= 0, jnp.exp(jnp.maximum(diff, 0.0) * log_gamma[:, None, None]), 0.0)
    zeta = jnp.exp((RET_CHUNK - 1 - idx) * log_gamma[:, None])
    xi = jnp.exp((idx + 1.0) * log_gamma[:, None])
    cd = jnp.exp(RET_CHUNK * log_gamma)
    xi_b = jnp.broadcast_to(xi[:, :, None], (RET_HEADS, RET_CHUNK, RET_DV))
    zeta_b = jnp.broadcast_to(zeta[:, :, None], (RET_HEADS, RET_CHUNK, RET_QK))
    cd_b = jnp.broadcast_to(cd[:, None, None], (RET_HEADS, 1, RET_DV))
    lane = jnp.arange(RET_QK)
    head_of_lane = (lane % (RET_QK // 2)) // (RET_DK // 2)
    mask = (head_of_lane[None, :] == jnp.arange(RET_HEADS)[:, None]).astype(F32)[:, None, :]
    return dec, xi_b, zeta_b, cd_b, mask


def _post_kernel(h_ref, o_ref, y_ref, pre_ref, wg_ref, gn_ref, wbm_ref, wbr_ref, wout_ref, post_ref,
                 out_ref):
    h = h_ref[...]
    hn = _rms(h, pre_ref[...]).astype(BF16)
    g = _dot(hn, wg_ref[...])
    rg = g[:, :RET_V]
    gate_mla = g[:, RET_V:RET_V + D_MODEL]
    gate_ret = g[:, RET_V + D_MODEL:]
    o_mla = _dot(o_ref[...], wbm_ref[...])
    a = (rg * jax.nn.sigmoid(rg)) * (y_ref[...] * gn_ref[...])
    o_ret = _dot(a.astype(BF16), wbr_ref[...])
    merged = jax.nn.sigmoid(gate_mla) * o_mla + jax.nn.sigmoid(gate_ret) * o_ret
    m = _dot(merged.astype(BF16), wout_ref[...])
    out_ref[...] = h + _rms(m, post_ref[...])


def _mixer_post(h, o, y, pre_w, w_gates, gn_w, w_bm, w_br, w_out, post_w, tm):
    t = h.shape[0]

    def row(width):
        return pl.BlockSpec((tm, width), lambda i: (i, 0))

    return pl.pallas_call(
        _post_kernel,
        grid=(t // tm,),
        in_specs=[row(D_MODEL), row(MLA_HEADS * MLA_V), row(RET_V), _const_spec((1, D_MODEL)),
                  _const_spec(w_gates.shape), _const_spec((1, RET_V)), _const_spec(w_bm.shape),
                  _const_spec(w_br.shape), _const_spec(w_out.shape), _const_spec((1, D_MODEL))],
        out_specs=row(D_MODEL),
        out_shape=jax.ShapeDtypeStruct((t, D_MODEL), F32),
        compiler_params=pltpu.CompilerParams(dimension_semantics=("parallel",),
                                             vmem_limit_bytes=VMEM_LIMIT),
        name="mixer_post",
    )(h, o, y, pre_w, w_gates, gn_w, w_bm, w_br, w_out, post_w)


def _rope_rows():
    lane = jnp.arange(LANES)
    half_m, half_r = MLA_ROPE // 2, RET_DK // 2
    f_mla = ROPE_BASE ** (-jnp.arange(half_m, dtype=F32) / half_m)
    f_ret = ROPE_BASE ** (-jnp.arange(half_r, dtype=F32) / half_r)
    freq = jnp.zeros((LANES,), F32)
    freq = freq.at[:half_r].set(f_ret)
    freq = freq.at[MLA_NOPE:MLA_NOPE + half_m].set(f_mla)
    freq = freq.at[MLA_NOPE + half_m:MLA_NOPE + 2 * half_m].set(f_mla)
    x1 = (lane >= MLA_NOPE) & (lane < MLA_NOPE + half_m)
    x2 = (lane >= MLA_NOPE + half_m) & (lane < MLA_NOPE + 2 * half_m)
    rows = [freq, (~(x1 | x2)).astype(F32), (x1 | x2).astype(F32), -x1.astype(F32), x2.astype(F32),
            (lane < half_r).astype(F32), jnp.zeros((LANES,), F32), jnp.zeros((LANES,), F32)]
    return jnp.stack(rows)


def _split_halves(w, heads, dim):
    w = w.reshape(w.shape[0], heads, 2, dim // 2)
    return jnp.concatenate([w[:, :, 0, :].reshape(w.shape[0], -1), w[:, :, 1, :].reshape(w.shape[0], -1)],
                           axis=1)


def _layout_w_in(w_in):
    sizes = (MLA_Q_RANK, MLA_KV_RANK, MLA_ROPE, RET_QK, RET_QK, RET_V, RET_V, D_MODEL, D_MODEL)
    parts, off = [], 0
    for s in sizes:
        parts.append(w_in[:, off:off + s])
        off += s
    w_cq, w_ckv, w_kr, w_rq, w_rk, w_rv, w_rg, w_gm, w_gr = parts
    rows = w_in.shape[0]
    w_kr = jnp.concatenate([jnp.zeros((rows, MLA_NOPE), w_in.dtype), w_kr,
                            jnp.zeros((rows, HEAD_PAD - MLA_NOPE - MLA_ROPE), w_in.dtype)], axis=1)
    w_pre = jnp.concatenate([w_cq, w_ckv, w_kr, _split_halves(w_rq, RET_HEADS, RET_DK),
                             _split_halves(w_rk, RET_HEADS, RET_DK), w_rv], axis=1)
    w_gates = jnp.concatenate([w_rg, w_gm, w_gr], axis=1)
    return w_pre.astype(BF16), w_gates.astype(BF16)


def _layout_w_uq(w_uq):
    w = w_uq.reshape(MLA_Q_RANK, MLA_HEADS, MLA_NOPE + MLA_ROPE)
    w = jnp.pad(w, ((0, 0), (0, 0), (0, HEAD_PAD - MLA_NOPE - MLA_ROPE)))
    return w.reshape(MLA_Q_RANK, MLA_HEADS * HEAD_PAD).astype(BF16)


def _layout_w_ukv(w_ukv):
    w = w_ukv.reshape(MLA_KV_RANK, MLA_HEADS, MLA_NOPE + MLA_V)
    w_k = jnp.pad(w[:, :, :MLA_NOPE], ((0, 0), (0, 0), (0, HEAD_PAD - MLA_NOPE)))
    w_v = jnp.pad(w[:, :, MLA_NOPE:], ((0, 0), (0, 0), (0, HEAD_PAD - MLA_V)))
    w_k = w_k.reshape(MLA_KV_RANK, -1).astype(BF16)
    w_vt = w_v.reshape(MLA_KV_RANK, -1).T.astype(BF16)
    return w_k, w_vt


def _row(w):
    return w.reshape(1, -1)


def kernel(x, positions, ffn1_pre_w, ffn1_w1, ffn1_w2, ffn1_post_w, mix_pre_w, w_in, mla_q_norm_w,
           mla_w_uq, mla_kv_norm_w, mla_w_ukv, ret_gn_w, w_branch_mla, w_branch_ret, w_out, mix_post_w,
           ffn2_pre_w, ffn2_w1, ffn2_w2, ffn2_post_w):
    batch, seq, _ = x.shape
    depth = ffn1_w1.shape[0]
    t = batch * seq
    tm = min(512, seq)
    tq = min(512, seq)
    tk = tq // 2
    h = x.reshape(t, D_MODEL)
    pos = positions.reshape(t, 1)
    rows = _rope_rows()
    ret_consts = _retention_consts()
    for l in range(depth):
        h = _ffn(h, _row(ffn1_pre_w[l]), ffn1_w1[l].astype(BF16), ffn1_w2[l].astype(BF16),
                 _row(ffn1_post_w[l]), tm)
        w_pre, w_gates = _layout_w_in(w_in[l])
        w_uk, w_uvt = _layout_w_ukv(mla_w_ukv[l])
        q, k, vt, rq, rk, rv = _mixer_pre(h, pos, rows, _row(mix_pre_w[l]), w_pre, _row(mla_q_norm_w[l]),
                                          _layout_w_uq(mla_w_uq[l]), _row(mla_kv_norm_w[l]), w_uk, w_uvt,
                                          batch, seq, tm, tk)
        o = _attention(q, k, vt, batch, seq, tq, tk)
        y = _retention(rq, rk, rv, ret_consts, batch, seq, tm)
        h = _mixer_post(h, o, y, _row(mix_pre_w[l]), w_gates, _row(ret_gn_w[l]),
                        w_branch_mla[l].astype(BF16), w_branch_ret[l].astype(BF16), w_out[l].astype(BF16),
                        _row(mix_post_w[l]), tm)
        h = _ffn(h, _row(ffn2_pre_w[l]), ffn2_w1[l].astype(BF16), ffn2_w2[l].astype(BF16),
                 _row(ffn2_post_w[l]), tm)
    return h.reshape(batch, seq, D_MODEL)
```

```python
import functools
import math

import jax
import jax.numpy as jnp
from jax import lax
from jax.experimental import pallas as pl
from jax.experimental.pallas import tpu as pltpu

D_MODEL = 1024
MLA_HEADS = 8
MLA_NOPE = 64
MLA_ROPE = 32
MLA_V = 64
MLA_Q_RANK = 384
MLA_KV_RANK = 256
RET_HEADS = 4
RET_DK = 64
RET_DV = 128
RET_CHUNK = 128
FF_DIM = 2816
ROPE_BASE = 10000.0
NORM_EPS = 1e-6
GN_EPS = 1e-6

LANES = 128
HEAD_PAD = 128
RET_QK = RET_HEADS * RET_DK
RET_V = RET_HEADS * RET_DV
VMEM_LIMIT = 56 * 1024 * 1024
FFN_PARTS = 4
POST_PARTS = 2

BF16 = jnp.bfloat16
F32 = jnp.float32

PRE_CQ = 0
PRE_CKV = PRE_CQ + MLA_Q_RANK
PRE_KR = PRE_CKV + MLA_KV_RANK
PRE_RQ = PRE_KR + HEAD_PAD
PRE_RK = PRE_RQ + RET_QK
PRE_RV = PRE_RK + RET_QK
PRE_WIDTH = PRE_RV + RET_V


def _rms(x, w):
    return x * lax.rsqrt(jnp.mean(x * x, axis=-1, keepdims=True) + NORM_EPS) * w


def _dot(a, b):
    return jnp.dot(a, b, preferred_element_type=F32)


def _dot_nt(a, b):
    return lax.dot_general(a, b, (((1,), (1,)), ((), ())), preferred_element_type=F32)


def _dot_tn(a, b):
    return lax.dot_general(a, b, (((0,), (0,)), ((), ())), preferred_element_type=F32)


def _const_spec(shape):
    nd = len(shape)
    return pl.BlockSpec(shape, lambda *_: (0,) * nd, pipeline_mode=pl.Buffered(1))


def _ffn_chunks():
    sizes, left = [], FF_DIM
    while left > 0:
        c = min(512, left)
        sizes.append(c)
        left -= c
    return tuple(sizes)


def _ffn_kernel(h_ref, pre_ref, w1_ref, w2_ref, post_ref, o_ref, *, parts):
    rows = h_ref.shape[0] // parts
    sl = [slice(p * rows, (p + 1) * rows) for p in range(parts)]
    xn = [_rms(h_ref[s, :], pre_ref[...]).astype(BF16) for s in sl]
    acc = [jnp.zeros((rows, D_MODEL), F32) for _ in sl]
    off = 0
    for c in _ffn_chunks():
        gu = [(_dot(x, w1_ref[:, off:off + c]), _dot(x, w1_ref[:, FF_DIM + off:FF_DIM + off + c]))
              for x in xn]
        for p, (g, u) in enumerate(gu):
            a = (g * jax.nn.sigmoid(g) * u).astype(BF16)
            acc[p] = acc[p] + _dot(a, w2_ref[off:off + c, :])
        off += c
    for p, s in enumerate(sl):
        o_ref[s, :] = h_ref[s, :] + 0.5 * _rms(acc[p], post_ref[...])


def _ffn(h, pre_w, w1, w2, post_w, tm):
    t = h.shape[0]
    row = pl.BlockSpec((tm, D_MODEL), lambda i: (i, 0))
    return pl.pallas_call(
        functools.partial(_ffn_kernel, parts=FFN_PARTS),
        grid=(t // tm,),
        in_specs=[row, _const_spec((1, D_MODEL)), _const_spec((D_MODEL, 2 * FF_DIM)),
                  _const_spec((FF_DIM, D_MODEL)), _const_spec((1, D_MODEL))],
        out_specs=row,
        out_shape=jax.ShapeDtypeStruct((t, D_MODEL), F32),
        compiler_params=pltpu.CompilerParams(dimension_semantics=("parallel",),
                                             vmem_limit_bytes=VMEM_LIMIT),
        name="ffn",
    )(h, pre_w, w1, w2, post_w)


def _rope_lanes(x, c, s):
    return x * c + pltpu.roll(x, LANES - MLA_ROPE, 1) * s


def _tile4(x):
    y = x + pltpu.roll(x, 32, 1)
    return y + pltpu.roll(y, 64, 1)


def _pre_kernel(h_ref, pos_ref, rows_ref, pre_ref, win_ref, qn_ref, wuq_ref, kvn_ref, wuk_ref, wuvt_ref,
                q_ref, k_ref, vt_ref, rq_ref, rk_ref, rv_ref, *, tk):
    parts = h_ref.shape[0] // tk
    sl = [slice(p * tk, (p + 1) * tk) for p in range(parts)]
    rows = rows_ref[...]
    half = RET_QK // 2
    k_scale = RET_DK ** -0.5

    hn = [_rms(h_ref[s, :], pre_ref[...]).astype(BF16) for s in sl]
    lat = [_dot(x, win_ref[:, :PRE_RQ]) for x in hn]

    up = []
    for p in range(parts):
        cq = _rms(lat[p][:, PRE_CQ:PRE_CKV], qn_ref[...]).astype(BF16)
        ckv = _rms(lat[p][:, PRE_CKV:PRE_KR], kvn_ref[...]).astype(BF16)
        up.append((_dot(cq, wuq_ref[...]), _dot(ckv, wuk_ref[...]), _dot_nt(wuvt_ref[...], ckv)))

    ret = [_dot(x, win_ref[:, PRE_RQ:]) for x in hn]

    for p, s in enumerate(sl):
        q_raw, k_raw, vt = up[p]
        lo = slice(p * tk, p * tk + tk // 2)
        hi = slice(p * tk + tk // 2, (p + 1) * tk)
        ang = (pos_ref[lo, :].astype(F32) * rows[0:1, :]
               + pos_ref[hi, :].astype(F32) * rows[4:5, :])
        cos, sin = jnp.cos(ang), jnp.sin(ang)
        tabs = []
        for m_mla, m_ret, shift in ((rows[2:3, :], rows[3:4, :], 32), (rows[5:6, :], rows[6:7, :], 96)):
            tabs.append((rows[1:2, :] + pltpu.roll(cos * m_mla, shift, 1),
                         pltpu.roll(sin * (m_mla * rows[7:8, :]), shift, 1),
                         _tile4(cos * m_ret), _tile4(sin * m_ret)))
        c_q, s_q, c_r, s_r = (jnp.concatenate([a, b], axis=0) for a, b in zip(*tabs))

        kr = _rope_lanes(lat[p][:, PRE_KR:PRE_RQ], c_q, s_q)
        for hd in range(MLA_HEADS):
            cols = slice(hd * HEAD_PAD, (hd + 1) * HEAD_PAD)
            q_ref[s, cols] = _rope_lanes(q_raw[:, cols], c_q, s_q).astype(BF16)
            k_ref[s, cols] = (k_raw[:, cols] + kr).astype(BF16)
        row = lax.broadcasted_iota(jnp.int32, vt.shape, 0)
        vt = jnp.where(row % HEAD_PAD == MLA_V, 1.0, vt).astype(BF16)
        for hd in range(MLA_HEADS):
            vt_ref[0, hd, p] = vt[hd * HEAD_PAD:(hd + 1) * HEAD_PAD, :]

        rq1, rq2 = ret[p][:, :half], ret[p][:, half:RET_QK]
        rq_ref[s, :half] = (rq1 * c_r - rq2 * s_r).astype(BF16)
        rq_ref[s, half:] = (rq2 * c_r + rq1 * s_r).astype(BF16)
        rk1, rk2 = ret[p][:, RET_QK:RET_QK + half], ret[p][:, RET_QK + half:2 * RET_QK]
        rk_ref[s, :half] = (rk1 * c_r - rk2 * s_r) * k_scale
        rk_ref[s, half:] = (rk2 * c_r + rk1 * s_r) * k_scale
        rv_ref[s, :] = ret[p][:, 2 * RET_QK:].astype(BF16)


def _mixer_pre(h, pos, rows, pre_w, w_in_pre, qn_w, w_uq, kvn_w, w_uk, w_uvt, batch, seq, tm, tk):
    t = h.shape[0]
    ns = seq // tm

    def row(width):
        return pl.BlockSpec((tm, width), lambda i: (i, 0))

    qk_w = MLA_HEADS * HEAD_PAD
    vt_shape = (batch, MLA_HEADS, seq // tk, HEAD_PAD, tk)
    vt_spec = pl.BlockSpec((1, MLA_HEADS, tm // tk, HEAD_PAD, tk), lambda i: (i // ns, 0, i % ns, 0, 0))
    return pl.pallas_call(
        functools.partial(_pre_kernel, tk=tk),
        grid=(t // tm,),
        in_specs=[row(D_MODEL), row(1), _const_spec((8, LANES)), _const_spec((1, D_MODEL)),
                  _const_spec((D_MODEL, PRE_WIDTH)), _const_spec((1, MLA_Q_RANK)),
                  _const_spec((MLA_Q_RANK, qk_w)), _const_spec((1, MLA_KV_RANK)),
                  _const_spec((MLA_KV_RANK, qk_w)), _const_spec((qk_w, MLA_KV_RANK))],
        out_specs=[row(qk_w), row(qk_w), vt_spec, row(RET_QK), row(RET_QK), row(RET_V)],
        out_shape=[jax.ShapeDtypeStruct((t, qk_w), BF16), jax.ShapeDtypeStruct((t, qk_w), BF16),
                   jax.ShapeDtypeStruct(vt_shape, BF16), jax.ShapeDtypeStruct((t, RET_QK), BF16),
                   jax.ShapeDtypeStruct((t, RET_QK), F32), jax.ShapeDtypeStruct((t, RET_V), BF16)],
        compiler_params=pltpu.CompilerParams(dimension_semantics=("parallel",),
                                             vmem_limit_bytes=VMEM_LIMIT),
        name="mixer_pre",
    )(h, pos, rows, pre_w, w_in_pre, qn_w, w_uq, kvn_w, w_uk, w_uvt)


def _attn_kernel(q_ref, k_ref, vt_ref, o_ref, sa_ref, sb_ref, *, tq, tk, c):
    qi = pl.program_id(2)
    assert tq == 2 * tk
    q = [q_ref[:, hd * HEAD_PAD:(hd + 1) * HEAD_PAD] for hd in range(2)]

    def qk(j, s_ref):
        start = pl.multiple_of(j * tk, tk)
        for hd in range(2):
            s_ref[hd] = _dot_nt(k_ref[pl.ds(start, tk), hd * HEAD_PAD:(hd + 1) * HEAD_PAD], q[hd])

    def softmax_pv(j, s_ref, stats, diag):
        out = []
        for hd in range(2):
            m, acc = stats[hd]
            if diag is None:
                keep = None
            else:
                key = lax.broadcasted_iota(jnp.int32, (tk, tq), 0) + diag * tk
                keep = key <= lax.broadcasted_iota(jnp.int32, (tk, tq), 1)

            def scores():
                s = s_ref[hd]
                return s if keep is None else jnp.where(keep, s, -jnp.inf)

            m_new = jnp.maximum(m, jnp.max(scores(), axis=0, keepdims=True) * c)
            alpha = jnp.exp2(m - m_new)
            p = jnp.exp2(scores() * c - m_new).astype(BF16)
            acc = alpha * acc + _dot(vt_ref[hd, j], p)
            out.append((m_new, acc))
        return tuple(out)

    def step(jj, stats):
        j = 2 * jj
        qk(j + 1, sb_ref)
        stats = softmax_pv(j, sa_ref, stats, None)
        qk(j + 2, sa_ref)
        return softmax_pv(j + 1, sb_ref, stats, None)

    stats = tuple((jnp.full((1, tq), -jnp.inf, F32), jnp.zeros((HEAD_PAD, tq), F32)) for _ in range(2))
    qk(0, sa_ref)
    stats = lax.fori_loop(0, qi, step, stats)
    qk(2 * qi + 1, sb_ref)
    stats = softmax_pv(2 * qi, sa_ref, stats, 0)
    stats = softmax_pv(2 * qi + 1, sb_ref, stats, 1)
    o_t = jnp.concatenate([acc[:MLA_V] / acc[MLA_V:MLA_V + 1] for _, acc in stats], axis=0)
    o_ref[...] = o_t.T.astype(o_ref.dtype)


def _attention(q, k, vt, batch, seq, tq, tk):
    t = q.shape[0]
    nq = seq // tq
    pairs = MLA_HEADS // 2
    c = math.log2(math.e) / math.sqrt(MLA_NOPE + MLA_ROPE)
    return pl.pallas_call(
        functools.partial(_attn_kernel, tq=tq, tk=tk, c=c),
        grid=(batch, pairs, nq),
        in_specs=[pl.BlockSpec((tq, 2 * HEAD_PAD), lambda b, p, i: (b * nq + i, p)),
                  pl.BlockSpec((seq, 2 * HEAD_PAD), lambda b, p, i: (b, p)),
                  pl.BlockSpec((None, 2, seq // tk, HEAD_PAD, tk), lambda b, p, i: (b, p, 0, 0, 0))],
        out_specs=pl.BlockSpec((tq, 2 * MLA_V), lambda b, p, i: (b * nq + i, p)),
        out_shape=jax.ShapeDtypeStruct((t, MLA_HEADS * MLA_V), BF16),
        compiler_params=pltpu.CompilerParams(
            dimension_semantics=("parallel", "parallel", "arbitrary"),
            vmem_limit_bytes=VMEM_LIMIT),
        scratch_shapes=[pltpu.VMEM((2, tk, tq), F32), pltpu.VMEM((2, tk, tq), F32)],
        name="attention",
    )(q, k, vt)


def _ret_kernel(rq_ref, rk_ref, rv_ref, dec_ref, xi_ref, zeta_ref, cd_ref, mask_ref, y_ref, state_ref,
                *, chunks):
    @pl.when(pl.program_id(1) == 0)
    def _():
        state_ref[...] = jnp.zeros(state_ref.shape, F32)

    blocks = [(c, hd) for c in range(chunks) for hd in range(RET_HEADS)]
    rows = [slice(c * RET_CHUNK, (c + 1) * RET_CHUNK) for c in range(chunks)]
    cols = [slice(hd * RET_DV, (hd + 1) * RET_DV) for hd in range(RET_HEADS)]
    q = [rq_ref[r, :] for r in rows]
    scores, kv = {}, {}
    for c, hd in blocks:
        km = rk_ref[rows[c], :] * mask_ref[hd]
        v = rv_ref[rows[c], cols[hd]]
        scores[c, hd] = _dot_nt(q[c], km.astype(BF16))
        kv[c, hd] = _dot_tn((km * zeta_ref[hd]).astype(BF16), v)
    prev = {}
    for hd in range(RET_HEADS):
        state = state_ref[hd]
        for c in range(chunks):
            prev[c, hd] = state
            state = state * cd_ref[hd] + kv[c, hd]
        state_ref[hd] = state
    for c, hd in blocks:
        inner = (scores[c, hd] * dec_ref[hd]).astype(BF16)
        y = (_dot(inner, rv_ref[rows[c], cols[hd]])
             + _dot(q[c], prev[c, hd].astype(BF16)) * xi_ref[hd])
        mu = jnp.mean(y, axis=-1, keepdims=True)
        yc = y - mu
        var = jnp.mean(yc * yc, axis=-1, keepdims=True)
        y_ref[rows[c], cols[hd]] = yc * lax.rsqrt(var + GN_EPS)


def _retention(rq, rk, rv, consts, batch, seq, tm):
    t = rq.shape[0]
    ns = seq // tm
    dec, xi, zeta, cd, mask = consts

    def row(width):
        return pl.BlockSpec((tm, width), lambda b, i: (b * ns + i, 0))

    return pl.pallas_call(
        functools.partial(_ret_kernel, chunks=tm // RET_CHUNK),
        grid=(batch, ns),
        in_specs=[row(RET_QK), row(RET_QK), row(RET_V), _const_spec(dec.shape), _const_spec(xi.shape),
                  _const_spec(zeta.shape), _const_spec(cd.shape), _const_spec(mask.shape)],
        out_specs=row(RET_V),
        out_shape=jax.ShapeDtypeStruct((t, RET_V), F32),
        scratch_shapes=[pltpu.VMEM((RET_HEADS, RET_QK, RET_DV), F32)],
        compiler_params=pltpu.CompilerParams(dimension_semantics=("parallel", "arbitrary"),
                                             vmem_limit_bytes=VMEM_LIMIT),
        name="retention",
    )(rq, rk, rv, dec, xi, zeta, cd, mask)


def _retention_consts():
    hh = jnp.arange(RET_HEADS, dtype=F32)
    log_gamma = jnp.log(1.0 - 2.0 ** (-5.0 - hh))
    idx = jnp.arange(RET_CHUNK, dtype=F32)
    diff = idx[:, None] - idx[None, :]
    dec = jnp.where(diff >= 0, jnp.exp(jnp.maximum(diff, 0.0) * log_gamma[:, None, None]), 0.0)
    zeta = jnp.exp((RET_CHUNK - 1 - idx) * log_gamma[:, None])
    xi = jnp.exp((idx + 1.0) * log_gamma[:, None])
    cd = jnp.exp(RET_CHUNK * log_gamma)
    xi_b = jnp.broadcast_to(xi[:, :, None], (RET_HEADS, RET_CHUNK, RET_DV))
    zeta_b = jnp.broadcast_to(zeta[:, :, None], (RET_HEADS, RET_CHUNK, RET_QK))
    cd_b = jnp.broadcast_to(cd[:, None, None], (RET_HEADS, 1, RET_DV))
    lane = jnp.arange(RET_QK)
    head_of_lane = (lane % (RET_QK // 2)) // (RET_DK // 2)
    mask = (head_of_lane[None, :] == jnp.arange(RET_HEADS)[:, None]).astype(F32)[:, None, :]
    return dec, xi_b, zeta_b, cd_b, mask


def _post_kernel(h_ref, o_ref, y_ref, pre_ref, wg_ref, gn_ref, wbm_ref, wbr_ref, wout_ref, post_ref,
                 out_ref, *, parts):
    rows = h_ref.shape[0] // parts
    sl = [slice(p * rows, (p + 1) * rows) for p in range(parts)]
    hn = [_rms(h_ref[s, :], pre_ref[...]).astype(BF16) for s in sl]
    rg = [_dot(x, wg_ref[:, :RET_V]) for x in hn]
    o_mla = [_dot(o_ref[s, :], wbm_ref[...]) for s in sl]
    gates = [_dot(x, wg_ref[:, RET_V:]) for x in hn]
    o_ret = []
    for p, s in enumerate(sl):
        a = (rg[p] * jax.nn.sigmoid(rg[p])) * (y_ref[s, :] * gn_ref[...])
        o_ret.append(_dot(a.astype(BF16), wbr_ref[...]))
    m = []
    for p in range(parts):
        merged = (jax.nn.sigmoid(gates[p][:, :D_MODEL]) * o_mla[p]
                  + jax.nn.sigmoid(gates[p][:, D_MODEL:]) * o_ret[p])
        m.append(_dot(merged.astype(BF16), wout_ref[...]))
    for p, s in enumerate(sl):
        out_ref[s, :] = h_ref[s, :] + _rms(m[p], post_ref[...])


def _mixer_post(h, o, y, pre_w, w_gates, gn_w, w_bm, w_br, w_out, post_w, tm):
    t = h.shape[0]

    def row(width):
        return pl.BlockSpec((tm, width), lambda i: (i, 0))

    return pl.pallas_call(
        functools.partial(_post_kernel, parts=POST_PARTS),
        grid=(t // tm,),
        in_specs=[row(D_MODEL), row(MLA_HEADS * MLA_V), row(RET_V), _const_spec((1, D_MODEL)),
                  _const_spec(w_gates.shape), _const_spec((1, RET_V)), _const_spec(w_bm.shape),
                  _const_spec(w_br.shape), _const_spec(w_out.shape), _const_spec((1, D_MODEL))],
        out_specs=row(D_MODEL),
        out_shape=jax.ShapeDtypeStruct((t, D_MODEL), F32),
        compiler_params=pltpu.CompilerParams(dimension_semantics=("parallel",),
                                             vmem_limit_bytes=VMEM_LIMIT),
        name="mixer_post",
    )(h, o, y, pre_w, w_gates, gn_w, w_bm, w_br, w_out, post_w)


def _rope_rows():
    lane = jnp.arange(LANES)
    half_m, half_r = MLA_ROPE // 2, RET_DK // 2
    f_mla = ROPE_BASE ** (-jnp.arange(half_m, dtype=F32) / half_m)
    f_ret = ROPE_BASE ** (-jnp.arange(half_r, dtype=F32) / half_r)
    token = jnp.concatenate([f_ret, f_mla, f_mla, jnp.zeros((LANES // 2,), F32)])
    in_tok = lane % (LANES // 2)
    m_ret = (in_tok < half_r).astype(F32)
    m_mla = (in_tok >= half_r).astype(F32)
    first = (lane < LANES // 2).astype(F32)
    sign = jnp.where(in_tok < half_r, 0.0, jnp.where(in_tok < half_r + half_m, -1.0, 1.0))
    rows = [token, (lane < MLA_NOPE).astype(F32), m_mla * first, m_ret * first,
            jnp.roll(token, LANES // 2), m_mla * (1 - first), m_ret * (1 - first), sign]
    return jnp.stack(rows)


def _split_halves(w, heads, dim):
    w = w.reshape(w.shape[0], heads, 2, dim // 2)
    return jnp.concatenate([w[:, :, 0, :].reshape(w.shape[0], -1), w[:, :, 1, :].reshape(w.shape[0], -1)],
                           axis=1)


def _rope_cols(w):
    h = MLA_ROPE // 2
    return jnp.concatenate([w, w[..., h:], w[..., :h]], axis=-1)


def _layout_w_in(w_in):
    sizes = (MLA_Q_RANK, MLA_KV_RANK, MLA_ROPE, RET_QK, RET_QK, RET_V, RET_V, D_MODEL, D_MODEL)
    parts, off = [], 0
    for s in sizes:
        parts.append(w_in[:, off:off + s])
        off += s
    w_cq, w_ckv, w_kr, w_rq, w_rk, w_rv, w_rg, w_gm, w_gr = parts
    rows = w_in.shape[0]
    w_kr = jnp.concatenate([jnp.zeros((rows, MLA_NOPE), w_in.dtype), _rope_cols(w_kr)], axis=1)
    w_pre = jnp.concatenate([w_cq, w_ckv, w_kr, _split_halves(w_rq, RET_HEADS, RET_DK),
                             _split_halves(w_rk, RET_HEADS, RET_DK), w_rv], axis=1)
    w_gates = jnp.concatenate([w_rg, w_gm, w_gr], axis=1)
    return w_pre.astype(BF16), w_gates.astype(BF16)


def _layout_w_uq(w_uq):
    w = w_uq.reshape(MLA_Q_RANK, MLA_HEADS, MLA_NOPE + MLA_ROPE)
    w = jnp.concatenate([w[..., :MLA_NOPE], _rope_cols(w[..., MLA_NOPE:])], axis=-1)
    return w.reshape(MLA_Q_RANK, MLA_HEADS * HEAD_PAD).astype(BF16)


def _layout_w_ukv(w_ukv):
    w = w_ukv.reshape(MLA_KV_RANK, MLA_HEADS, MLA_NOPE + MLA_V)
    w_k = jnp.pad(w[:, :, :MLA_NOPE], ((0, 0), (0, 0), (0, HEAD_PAD - MLA_NOPE)))
    w_v = jnp.pad(w[:, :, MLA_NOPE:], ((0, 0), (0, 0), (0, HEAD_PAD - MLA_V)))
    w_k = w_k.reshape(MLA_KV_RANK, -1).astype(BF16)
    w_vt = w_v.reshape(MLA_KV_RANK, -1).T.astype(BF16)
    return w_k, w_vt


def _row(w):
    return w.reshape(1, -1)


def kernel(x, positions, ffn1_pre_w, ffn1_w1, ffn1_w2, ffn1_post_w, mix_pre_w, w_in, mla_q_norm_w,
           mla_w_uq, mla_kv_norm_w, mla_w_ukv, ret_gn_w, w_branch_mla, w_branch_ret, w_out, mix_post_w,
           ffn2_pre_w, ffn2_w1, ffn2_w2, ffn2_post_w):
    batch, seq, _ = x.shape
    depth = ffn1_w1.shape[0]
    t = batch * seq
    tm = min(512, seq)
    tm_ffn = min(1024, seq)
    tq = min(512, seq)
    tk = tq // 2
    h = x.reshape(t, D_MODEL)
    pos = positions.reshape(t, 1)
    rows = _rope_rows()
    ret_consts = _retention_consts()
    for l in range(depth):
        h = _ffn(h, _row(ffn1_pre_w[l]), ffn1_w1[l].astype(BF16), ffn1_w2[l].astype(BF16),
                 _row(ffn1_post_w[l]), tm_ffn)
        w_pre, w_gates = _layout_w_in(w_in[l])
        w_uk, w_uvt = _layout_w_ukv(mla_w_ukv[l])
        q, k, vt, rq, rk, rv = _mixer_pre(h, pos, rows, _row(mix_pre_w[l]), w_pre, _row(mla_q_norm_w[l]),
                                          _layout_w_uq(mla_w_uq[l]), _row(mla_kv_norm_w[l]), w_uk, w_uvt,
                                          batch, seq, tm, tk)
        o = _attention(q, k, vt, batch, seq, tq, tk)
        y = _retention(rq, rk, rv, ret_consts, batch, seq, tm)
        h = _mixer_post(h, o, y, _row(mix_pre_w[l]), w_gates, _row(ret_gn_w[l]),
                        w_branch_mla[l].astype(BF16), w_branch_ret[l].astype(BF16), w_out[l].astype(BF16),
                        _row(mix_post_w[l]), tm)
        h = _ffn(h, _row(ffn2_pre_w[l]), ffn2_w1[l].astype(BF16), ffn2_w2[l].astype(BF16),
                 _row(ffn2_post_w[l]), tm_ffn)
    return h.reshape(batch, seq, D_MODEL)
```

```python
import functools
import math

import jax
import jax.numpy as jnp
from jax import lax
from jax.experimental import pallas as pl
from jax.experimental.pallas import tpu as pltpu

D_MODEL = 1024
MLA_HEADS = 8
MLA_NOPE = 64
MLA_ROPE = 32
MLA_V = 64
MLA_Q_RANK = 384
MLA_KV_RANK = 256
RET_HEADS = 4
RET_DK = 64
RET_DV = 128
RET_CHUNK = 128
FF_DIM = 2816
ROPE_BASE = 10000.0
NORM_EPS = 1e-6
GN_EPS = 1e-6

LANES = 128
HEAD_PAD = 128
RET_QK = RET_HEADS * RET_DK
RET_V = RET_HEADS * RET_DV
VMEM_LIMIT = 56 * 1024 * 1024
FFN_PARTS = 4
POST_PARTS = 2
ATTN_HEADS_PER_STEP = 4

ATTN_LOG2_SCALE = math.log2(math.e) / math.sqrt(MLA_NOPE + MLA_ROPE)

BF16 = jnp.bfloat16
F32 = jnp.float32

PRE_CQ = 0
PRE_CKV = PRE_CQ + MLA_Q_RANK
PRE_KR = PRE_CKV + MLA_KV_RANK
PRE_RQ = PRE_KR + HEAD_PAD
PRE_RK = PRE_RQ + RET_QK
PRE_RV = PRE_RK + RET_QK
PRE_WIDTH = PRE_RV + RET_V


def _rms(x, w):
    return x * lax.rsqrt(jnp.mean(x * x, axis=-1, keepdims=True) + NORM_EPS) * w


def _dot(a, b):
    return jnp.dot(a, b, preferred_element_type=F32)


def _dot_nt(a, b):
    return lax.dot_general(a, b, (((1,), (1,)), ((), ())), preferred_element_type=F32)


def _dot_tn(a, b):
    return lax.dot_general(a, b, (((0,), (0,)), ((), ())), preferred_element_type=F32)


def _const_spec(shape):
    nd = len(shape)
    return pl.BlockSpec(shape, lambda *_: (0,) * nd, pipeline_mode=pl.Buffered(1))


def _ffn_chunks():
    sizes, left = [], FF_DIM
    while left > 0:
        c = min(512, left)
        sizes.append(c)
        left -= c
    return tuple(sizes)


def _ffn_kernel(h_ref, pre_ref, w1_ref, w2_ref, post_ref, o_ref, *, parts):
    rows = h_ref.shape[0] // parts
    sl = [slice(p * rows, (p + 1) * rows) for p in range(parts)]
    xn = [_rms(h_ref[s, :], pre_ref[...]).astype(BF16) for s in sl]
    acc = [jnp.zeros((rows, D_MODEL), F32) for _ in sl]
    off = 0
    for c in _ffn_chunks():
        gu = [(_dot(x, w1_ref[:, off:off + c]), _dot(x, w1_ref[:, FF_DIM + off:FF_DIM + off + c]))
              for x in xn]
        for p, (g, u) in enumerate(gu):
            a = (g * jax.nn.sigmoid(g) * u).astype(BF16)
            acc[p] = acc[p] + _dot(a, w2_ref[off:off + c, :])
        off += c
    for p, s in enumerate(sl):
        o_ref[s, :] = h_ref[s, :] + 0.5 * _rms(acc[p], post_ref[...])


def _ffn(h, pre_w, w1, w2, post_w, tm):
    t = h.shape[0]
    row = pl.BlockSpec((tm, D_MODEL), lambda i: (i, 0))
    return pl.pallas_call(
        functools.partial(_ffn_kernel, parts=FFN_PARTS),
        grid=(t // tm,),
        in_specs=[row, _const_spec((1, D_MODEL)), _const_spec((D_MODEL, 2 * FF_DIM)),
                  _const_spec((FF_DIM, D_MODEL)), _const_spec((1, D_MODEL))],
        out_specs=row,
        out_shape=jax.ShapeDtypeStruct((t, D_MODEL), F32),
        compiler_params=pltpu.CompilerParams(dimension_semantics=("parallel",),
                                             vmem_limit_bytes=VMEM_LIMIT),
        name="ffn",
    )(h, pre_w, w1, w2, post_w)


def _rope_lanes(x, c, s):
    return x * c + pltpu.roll(x, LANES - MLA_ROPE, 1) * s


def _tile4(x):
    y = x + pltpu.roll(x, 32, 1)
    return y + pltpu.roll(y, 64, 1)


def _pre_kernel(h_ref, pos_ref, rows_ref, pre_ref, win_ref, qn_ref, wuq_ref, kvn_ref, wuk_ref, wuvt_ref,
                q_ref, k_ref, vt_ref, rq_ref, rk_ref, rv_ref, *, tk):
    parts = h_ref.shape[0] // tk
    sl = [slice(p * tk, (p + 1) * tk) for p in range(parts)]
    rows = rows_ref[...]
    half = RET_QK // 2
    k_scale = RET_DK ** -0.5

    hn = [_rms(h_ref[s, :], pre_ref[...]).astype(BF16) for s in sl]
    lat = [_dot(x, win_ref[:, :PRE_RQ]) for x in hn]

    up = []
    for p in range(parts):
        cq = _rms(lat[p][:, PRE_CQ:PRE_CKV], qn_ref[...]).astype(BF16)
        ckv = _rms(lat[p][:, PRE_CKV:PRE_KR], kvn_ref[...]).astype(BF16)
        up.append((_dot(cq, wuq_ref[...]), _dot(ckv, wuk_ref[...]), _dot_nt(wuvt_ref[...], ckv)))

    ret = [_dot(x, win_ref[:, PRE_RQ:]) for x in hn]

    for p, s in enumerate(sl):
        q_raw, k_raw, vt = up[p]
        lo = slice(p * tk, p * tk + tk // 2)
        hi = slice(p * tk + tk // 2, (p + 1) * tk)
        ang = (pos_ref[lo, :].astype(F32) * rows[0:1, :]
               + pos_ref[hi, :].astype(F32) * rows[4:5, :])
        cos, sin = jnp.cos(ang), jnp.sin(ang)
        tabs = []
        for m_mla, m_ret, shift in ((rows[2:3, :], rows[3:4, :], 32), (rows[5:6, :], rows[6:7, :], 96)):
            tabs.append((rows[1:2, :] + pltpu.roll(cos * m_mla, shift, 1),
                         pltpu.roll(sin * (m_mla * rows[7:8, :]), shift, 1),
                         _tile4(cos * m_ret), _tile4(sin * m_ret)))
        c_q, s_q, c_r, s_r = (jnp.concatenate([a, b], axis=0) for a, b in zip(*tabs))

        kr = _rope_lanes(lat[p][:, PRE_KR:PRE_RQ], c_q, s_q)
        for hd in range(MLA_HEADS):
            cols = slice(hd * HEAD_PAD, (hd + 1) * HEAD_PAD)
            q_ref[s, cols] = (_rope_lanes(q_raw[:, cols], c_q, s_q) * ATTN_LOG2_SCALE).astype(BF16)
            k_ref[s, cols] = (k_raw[:, cols] + kr).astype(BF16)
        row = lax.broadcasted_iota(jnp.int32, vt.shape, 0)
        vt = jnp.where(row % HEAD_PAD == MLA_V, 1.0, vt).astype(BF16)
        for hd in range(MLA_HEADS):
            vt_ref[0, hd, p] = vt[hd * HEAD_PAD:(hd + 1) * HEAD_PAD, :]

        rq1, rq2 = ret[p][:, :half], ret[p][:, half:RET_QK]
        rq_ref[s, :half] = (rq1 * c_r - rq2 * s_r).astype(BF16)
        rq_ref[s, half:] = (rq2 * c_r + rq1 * s_r).astype(BF16)
        rk1, rk2 = ret[p][:, RET_QK:RET_QK + half], ret[p][:, RET_QK + half:2 * RET_QK]
        rk_ref[s, :half] = (rk1 * c_r - rk2 * s_r) * k_scale
        rk_ref[s, half:] = (rk2 * c_r + rk1 * s_r) * k_scale
        rv_ref[s, :] = ret[p][:, 2 * RET_QK:].astype(BF16)


def _mixer_pre(h, pos, rows, pre_w, w_in_pre, qn_w, w_uq, kvn_w, w_uk, w_uvt, batch, seq, tm, tk):
    t = h.shape[0]
    ns = seq // tm

    def row(width):
        return pl.BlockSpec((tm, width), lambda i: (i, 0))

    qk_w = MLA_HEADS * HEAD_PAD
    vt_shape = (batch, MLA_HEADS, seq // tk, HEAD_PAD, tk)
    vt_spec = pl.BlockSpec((1, MLA_HEADS, tm // tk, HEAD_PAD, tk), lambda i: (i // ns, 0, i % ns, 0, 0))
    return pl.pallas_call(
        functools.partial(_pre_kernel, tk=tk),
        grid=(t // tm,),
        in_specs=[row(D_MODEL), row(1), _const_spec((8, LANES)), _const_spec((1, D_MODEL)),
                  _const_spec((D_MODEL, PRE_WIDTH)), _const_spec((1, MLA_Q_RANK)),
                  _const_spec((MLA_Q_RANK, qk_w)), _const_spec((1, MLA_KV_RANK)),
                  _const_spec((MLA_KV_RANK, qk_w)), _const_spec((qk_w, MLA_KV_RANK))],
        out_specs=[row(qk_w), row(qk_w), vt_spec, row(RET_QK), row(RET_QK), row(RET_V)],
        out_shape=[jax.ShapeDtypeStruct((t, qk_w), BF16), jax.ShapeDtypeStruct((t, qk_w), BF16),
                   jax.ShapeDtypeStruct(vt_shape, BF16), jax.ShapeDtypeStruct((t, RET_QK), BF16),
                   jax.ShapeDtypeStruct((t, RET_QK), F32), jax.ShapeDtypeStruct((t, RET_V), BF16)],
        compiler_params=pltpu.CompilerParams(dimension_semantics=("parallel",),
                                             vmem_limit_bytes=VMEM_LIMIT),
        name="mixer_pre",
    )(h, pos, rows, pre_w, w_in_pre, qn_w, w_uq, kvn_w, w_uk, w_uvt)


def _attn_kernel(q_ref, k_ref, vt_ref, o_ref, sa_ref, sb_ref, *, tq, tk):
    qi = pl.program_id(2)
    assert tq == 2 * tk
    heads = q_ref.shape[1] // HEAD_PAD
    q = [q_ref[:, hd * HEAD_PAD:(hd + 1) * HEAD_PAD] for hd in range(heads)]

    def qk(j, s_ref, q_lo=0):
        start = pl.multiple_of(j * tk, tk)
        for hd in range(heads):
            s_ref[hd, :, q_lo:] = _dot_nt(k_ref[pl.ds(start, tk), hd * HEAD_PAD:(hd + 1) * HEAD_PAD],
                                          q[hd][q_lo:, :])

    def softmax_pv(j, s_ref, stats, q_lo=0, masked=False):
        out = []
        for hd in range(heads):
            m_all, acc_all = stats[hd]
            m, acc = m_all[:, q_lo:], acc_all[:, q_lo:]

            def scores():
                s = s_ref[hd, :, q_lo:]
                if masked:
                    keep = (lax.broadcasted_iota(jnp.int32, s.shape, 0)
                            <= lax.broadcasted_iota(jnp.int32, s.shape, 1))
                    s = jnp.where(keep, s, -jnp.inf)
                return s

            m_new = jnp.maximum(m, jnp.max(scores(), axis=0, keepdims=True))
            alpha = jnp.exp2(m - m_new)
            p = jnp.exp2(scores() - m_new).astype(BF16)
            acc = alpha * acc + _dot(vt_ref[hd, j], p)
            if q_lo:
                m_new = jnp.concatenate([m_all[:, :q_lo], m_new], axis=1)
                acc = jnp.concatenate([acc_all[:, :q_lo], acc], axis=1)
            out.append((m_new, acc))
        return tuple(out)

    def step(jj, stats):
        j = 2 * jj
        qk(j + 1, sb_ref)
        stats = softmax_pv(j, sa_ref, stats)
        qk(j + 2, sa_ref)
        return softmax_pv(j + 1, sb_ref, stats)

    stats = tuple((jnp.full((1, tq), -jnp.inf, F32), jnp.zeros((HEAD_PAD, tq), F32))
                  for _ in range(heads))
    qk(0, sa_ref)
    stats = lax.fori_loop(0, qi, step, stats)
    qk(2 * qi + 1, sb_ref, q_lo=tk)
    stats = softmax_pv(2 * qi, sa_ref, stats, masked=True)
    stats = softmax_pv(2 * qi + 1, sb_ref, stats, q_lo=tk, masked=True)
    o_t = jnp.concatenate([acc[:MLA_V] / acc[MLA_V:MLA_V + 1] for _, acc in stats], axis=0)
    o_ref[...] = o_t.T.astype(o_ref.dtype)


def _attention(q, k, vt, batch, seq, tq, tk):
    t = q.shape[0]
    nq = seq // tq
    hg = ATTN_HEADS_PER_STEP
    return pl.pallas_call(
        functools.partial(_attn_kernel, tq=tq, tk=tk),
        grid=(batch, MLA_HEADS // hg, nq),
        in_specs=[pl.BlockSpec((tq, hg * HEAD_PAD), lambda b, p, i: (b * nq + i, p)),
                  pl.BlockSpec((seq, hg * HEAD_PAD), lambda b, p, i: (b, p)),
                  pl.BlockSpec((None, hg, seq // tk, HEAD_PAD, tk), lambda b, p, i: (b, p, 0, 0, 0))],
        out_specs=pl.BlockSpec((tq, hg * MLA_V), lambda b, p, i: (b * nq + i, p)),
        out_shape=jax.ShapeDtypeStruct((t, MLA_HEADS * MLA_V), BF16),
        compiler_params=pltpu.CompilerParams(
            dimension_semantics=("parallel", "parallel", "arbitrary"),
            vmem_limit_bytes=VMEM_LIMIT),
        scratch_shapes=[pltpu.VMEM((hg, tk, tq), F32), pltpu.VMEM((hg, tk, tq), F32)],
        name="attention",
    )(q, k, vt)


def _ret_kernel(rq_ref, rk_ref, rv_ref, dec_ref, xi_ref, zeta_ref, cd_ref, mask_ref, y_ref, state_ref,
                *, chunks):
    @pl.when(pl.program_id(1) == 0)
    def _():
        state_ref[...] = jnp.zeros(state_ref.shape, F32)

    blocks = [(c, hd) for c in range(chunks) for hd in range(RET_HEADS)]
    rows = [slice(c * RET_CHUNK, (c + 1) * RET_CHUNK) for c in range(chunks)]
    cols = [slice(hd * RET_DV, (hd + 1) * RET_DV) for hd in range(RET_HEADS)]
    q = [rq_ref[r, :] for r in rows]
    scores, kv = {}, {}
    for c, hd in blocks:
        km = rk_ref[rows[c], :] * mask_ref[hd]
        v = rv_ref[rows[c], cols[hd]]
        scores[c, hd] = _dot_nt(q[c], km.astype(BF16))
        kv[c, hd] = _dot_tn((km * zeta_ref[hd]).astype(BF16), v)
    prev = {}
    for hd in range(RET_HEADS):
        state = state_ref[hd]
        for c in range(chunks):
            prev[c, hd] = state
            state = state * cd_ref[hd] + kv[c, hd]
        state_ref[hd] = state
    for c, hd in blocks:
        inner = (scores[c, hd] * dec_ref[hd]).astype(BF16)
        y = (_dot(inner, rv_ref[rows[c], cols[hd]])
             + _dot(q[c], prev[c, hd].astype(BF16)) * xi_ref[hd])
        mu = jnp.mean(y, axis=-1, keepdims=True)
        yc = y - mu
        var = jnp.mean(yc * yc, axis=-1, keepdims=True)
        y_ref[rows[c], cols[hd]] = yc * lax.rsqrt(var + GN_EPS)


def _retention(rq, rk, rv, consts, batch, seq, tm):
    t = rq.shape[0]
    ns = seq // tm
    dec, xi, zeta, cd, mask = consts

    def row(width):
        return pl.BlockSpec((tm, width), lambda b, i: (b * ns + i, 0))

    return pl.pallas_call(
        functools.partial(_ret_kernel, chunks=tm // RET_CHUNK),
        grid=(batch, ns),
        in_specs=[row(RET_QK), row(RET_QK), row(RET_V), _const_spec(dec.shape), _const_spec(xi.shape),
                  _const_spec(zeta.shape), _const_spec(cd.shape), _const_spec(mask.shape)],
        out_specs=row(RET_V),
        out_shape=jax.ShapeDtypeStruct((t, RET_V), F32),
        scratch_shapes=[pltpu.VMEM((RET_HEADS, RET_QK, RET_DV), F32)],
        compiler_params=pltpu.CompilerParams(dimension_semantics=("parallel", "arbitrary"),
                                             vmem_limit_bytes=VMEM_LIMIT),
        name="retention",
    )(rq, rk, rv, dec, xi, zeta, cd, mask)


def _retention_consts():
    hh = jnp.arange(RET_HEADS, dtype=F32)
    log_gamma = jnp.log(1.0 - 2.0 ** (-5.0 - hh))
    idx = jnp.arange(RET_CHUNK, dtype=F32)
    diff = idx[:, None] - idx[None, :]
    dec = jnp.where(diff >= 0, jnp.exp(jnp.maximum(diff, 0.0) * log_gamma[:, None, None]), 0.0)
    zeta = jnp.exp((RET_CHUNK - 1 - idx) * log_gamma[:, None])
    xi = jnp.exp((idx + 1.0) * log_gamma[:, None])
    cd = jnp.exp(RET_CHUNK * log_gamma)
    xi_b = jnp.broadcast_to(xi[:, :, None], (RET_HEADS, RET_CHUNK, RET_DV))
    zeta_b = jnp.broadcast_to(zeta[:, :, None], (RET_HEADS, RET_CHUNK, RET_QK))
    cd_b = jnp.broadcast_to(cd[:, None, None], (RET_HEADS, 1, RET_DV))
    lane = jnp.arange(RET_QK)
    head_of_lane = (lane % (RET_QK // 2)) // (RET_DK // 2)
    mask = (head_of_lane[None, :] == jnp.arange(RET_HEADS)[:, None]).astype(F32)[:, None, :]
    return dec, xi_b, zeta_b, cd_b, mask


def _post_kernel(h_ref, o_ref, y_ref, pre_ref, wg_ref, gn_ref, wbm_ref, wbr_ref, wout_ref, post_ref,
                 out_ref, *, parts):
    rows = h_ref.shape[0] // parts
    sl = [slice(p * rows, (p + 1) * rows) for p in range(parts)]
    hn = [_rms(h_ref[s, :], pre_ref[...]).astype(BF16) for s in sl]
    rg = [_dot(x, wg_ref[:, :RET_V]) for x in hn]
    o_mla = [_dot(o_ref[s, :], wbm_ref[...]) for s in sl]
    gates = [_dot(x, wg_ref[:, RET_V:]) for x in hn]
    o_ret = []
    for p, s in enumerate(sl):
        a = (rg[p] * jax.nn.sigmoid(rg[p])) * (y_ref[s, :] * gn_ref[...])
        o_ret.append(_dot(a.astype(BF16), wbr_ref[...]))
    m = []
    for p in range(parts):
        merged = (jax.nn.sigmoid(gates[p][:, :D_MODEL]) * o_mla[p]
                  + jax.nn.sigmoid(gates[p][:, D_MODEL:]) * o_ret[p])
        m.append(_dot(merged.astype(BF16), wout_ref[...]))
    for p, s in enumerate(sl):
        out_ref[s, :] = h_ref[s, :] + _rms(m[p], post_ref[...])


def _mixer_post(h, o, y, pre_w, w_gates, gn_w, w_bm, w_br, w_out, post_w, tm):
    t = h.shape[0]

    def row(width):
        return pl.BlockSpec((tm, width), lambda i: (i, 0))

    return pl.pallas_call(
        functools.partial(_post_kernel, parts=POST_PARTS),
        grid=(t // tm,),
        in_specs=[row(D_MODEL), row(MLA_HEADS * MLA_V), row(RET_V), _const_spec((1, D_MODEL)),
                  _const_spec(w_gates.shape), _const_spec((1, RET_V)), _const_spec(w_bm.shape),
                  _const_spec(w_br.shape), _const_spec(w_out.shape), _const_spec((1, D_MODEL))],
        out_specs=row(D_MODEL),
        out_shape=jax.ShapeDtypeStruct((t, D_MODEL), F32),
        compiler_params=pltpu.CompilerParams(dimension_semantics=("parallel",),
                                             vmem_limit_bytes=VMEM_LIMIT),
        name="mixer_post",
    )(h, o, y, pre_w, w_gates, gn_w, w_bm, w_br, w_out, post_w)


def _rope_rows():
    lane = jnp.arange(LANES)
    half_m, half_r = MLA_ROPE // 2, RET_DK // 2
    f_mla = ROPE_BASE ** (-jnp.arange(half_m, dtype=F32) / half_m)
    f_ret = ROPE_BASE ** (-jnp.arange(half_r, dtype=F32) / half_r)
    token = jnp.concatenate([f_ret, f_mla, f_mla, jnp.zeros((LANES // 2,), F32)])
    in_tok = lane % (LANES // 2)
    m_ret = (in_tok < half_r).astype(F32)
    m_mla = (in_tok >= half_r).astype(F32)
    first = (lane < LANES // 2).astype(F32)
    sign = jnp.where(in_tok < half_r, 0.0, jnp.where(in_tok < half_r + half_m, -1.0, 1.0))
    rows = [token, (lane < MLA_NOPE).astype(F32), m_mla * first, m_ret * first,
            jnp.roll(token, LANES // 2), m_mla * (1 - first), m_ret * (1 - first), sign]
    return jnp.stack(rows)


def _split_halves(w, heads, dim):
    h = dim // 2
    return ([w[:, hd * dim:hd * dim + h] for hd in range(heads)]
            + [w[:, hd * dim + h:(hd + 1) * dim] for hd in range(heads)])


def _rope_cols(w):
    h = MLA_ROPE // 2
    return jnp.concatenate([w, w[..., h:], w[..., :h]], axis=-1)


def _layout_w_in(w_in):
    sizes = (MLA_Q_RANK, MLA_KV_RANK, MLA_ROPE, RET_QK, RET_QK, RET_V, RET_V, D_MODEL, D_MODEL)
    parts, off = [], 0
    for s in sizes:
        parts.append(w_in[:, off:off + s])
        off += s
    w_cq, w_ckv, w_kr, w_rq, w_rk, w_rv = parts[:6]
    rows = w_in.shape[0]
    w_pre = jnp.concatenate([w_cq, w_ckv, jnp.zeros((rows, MLA_NOPE), w_in.dtype), _rope_cols(w_kr)]
                            + _split_halves(w_rq, RET_HEADS, RET_DK)
                            + _split_halves(w_rk, RET_HEADS, RET_DK) + [w_rv], axis=1)
    w_gates = w_in[:, w_in.shape[1] - (RET_V + 2 * D_MODEL):]
    return w_pre.astype(BF16), w_gates.astype(BF16)


def _layout_w_uq(w_uq):
    w = w_uq.reshape(MLA_Q_RANK, MLA_HEADS, MLA_NOPE + MLA_ROPE)
    w = jnp.concatenate([w[..., :MLA_NOPE], _rope_cols(w[..., MLA_NOPE:])], axis=-1)
    return w.reshape(MLA_Q_RANK, MLA_HEADS * HEAD_PAD).astype(BF16)


def _layout_w_ukv(w_ukv):
    w = w_ukv.reshape(MLA_KV_RANK, MLA_HEADS, MLA_NOPE + MLA_V)
    w_k = jnp.pad(w[:, :, :MLA_NOPE], ((0, 0), (0, 0), (0, HEAD_PAD - MLA_NOPE)))
    w_v = jnp.pad(w[:, :, MLA_NOPE:], ((0, 0), (0, 0), (0, HEAD_PAD - MLA_V)))
    w_k = w_k.reshape(MLA_KV_RANK, -1).astype(BF16)
    w_vt = w_v.reshape(MLA_KV_RANK, -1).T.astype(BF16)
    return w_k, w_vt


def _row(w):
    return w.reshape(1, -1)


def kernel(x, positions, ffn1_pre_w, ffn1_w1, ffn1_w2, ffn1_post_w, mix_pre_w, w_in, mla_q_norm_w,
           mla_w_uq, mla_kv_norm_w, mla_w_ukv, ret_gn_w, w_branch_mla, w_branch_ret, w_out, mix_post_w,
           ffn2_pre_w, ffn2_w1, ffn2_w2, ffn2_post_w):
    batch, seq, _ = x.shape
    depth = ffn1_w1.shape[0]
    t = batch * seq
    tm = min(512, seq)
    tm_ffn = min(1024, seq)
    tq = min(512, seq)
    tk = tq // 2
    h = x.reshape(t, D_MODEL)
    pos = positions.reshape(t, 1)
    rows = _rope_rows()
    ret_consts = _retention_consts()
    for l in range(depth):
        h = _ffn(h, _row(ffn1_pre_w[l]), ffn1_w1[l].astype(BF16), ffn1_w2[l].astype(BF16),
                 _row(ffn1_post_w[l]), tm_ffn)
        w_pre, w_gates = _layout_w_in(w_in[l])
        w_uk, w_uvt = _layout_w_ukv(mla_w_ukv[l])
        q, k, vt, rq, rk, rv = _mixer_pre(h, pos, rows, _row(mix_pre_w[l]), w_pre, _row(mla_q_norm_w[l]),
                                          _layout_w_uq(mla_w_uq[l]), _row(mla_kv_norm_w[l]), w_uk, w_uvt,
                                          batch, seq, tm, tk)
        o = _attention(q, k, vt, batch, seq, tq, tk)
        y = _retention(rq, rk, rv, ret_consts, batch, seq, tm)
        h = _mixer_post(h, o, y, _row(mix_pre_w[l]), w_gates, _row(ret_gn_w[l]),
                        w_branch_mla[l].astype(BF16), w_branch_ret[l].astype(BF16), w_out[l].astype(BF16),
                        _row(mix_post_w[l]), tm)
        h = _ffn(h, _row(ffn2_pre_w[l]), ffn2_w1[l].astype(BF16), ffn2_w2[l].astype(BF16),
                 _row(ffn2_post_w[l]), tm_ffn)
    return h.reshape(batch, seq, D_MODEL)
```

```python
import functools
import math

import jax
import jax.numpy as jnp
from jax import lax
from jax.experimental import pallas as pl
from jax.experimental.pallas import tpu as pltpu

D_MODEL = 1024
MLA_HEADS = 8
MLA_NOPE = 64
MLA_ROPE = 32
MLA_V = 64
MLA_Q_RANK = 384
MLA_KV_RANK = 256
RET_HEADS = 4
RET_DK = 64
RET_DV = 128
RET_CHUNK = 128
FF_DIM = 2816
ROPE_BASE = 10000.0
NORM_EPS = 1e-6
GN_EPS = 1e-6

LANES = 128
HEAD_PAD = 128
RET_QK = RET_HEADS * RET_DK
RET_V = RET_HEADS * RET_DV
VMEM_LIMIT = 56 * 1024 * 1024
FFN_PARTS = 4
POST_PARTS = 2
ATTN_HEADS_PER_STEP = 4

ATTN_LOG2_SCALE = math.log2(math.e) / math.sqrt(MLA_NOPE + MLA_ROPE)

BF16 = jnp.bfloat16
F32 = jnp.float32

PRE_CQ = 0
PRE_CKV = PRE_CQ + MLA_Q_RANK
PRE_KR = PRE_CKV + MLA_KV_RANK
PRE_RQ = PRE_KR + HEAD_PAD
PRE_RK = PRE_RQ + RET_QK
PRE_RV = PRE_RK + RET_QK
PRE_WIDTH = PRE_RV + RET_V


def _rms(x, w):
    return x * lax.rsqrt(jnp.mean(x * x, axis=-1, keepdims=True) + NORM_EPS) * w


def _dot(a, b):
    return jnp.dot(a, b, preferred_element_type=F32)


def _dot_nt(a, b):
    return lax.dot_general(a, b, (((1,), (1,)), ((), ())), preferred_element_type=F32)


def _dot_tn(a, b):
    return lax.dot_general(a, b, (((0,), (0,)), ((), ())), preferred_element_type=F32)


def _const_spec(shape):
    nd = len(shape)
    return pl.BlockSpec(shape, lambda *_: (0,) * nd, pipeline_mode=pl.Buffered(1))


def _ffn_chunks():
    sizes, left = [], FF_DIM
    while left > 0:
        c = min(512, left)
        sizes.append(c)
        left -= c
    return tuple(sizes)


def _ffn_kernel(h_ref, pre_ref, w1_ref, w2_ref, post_ref, o_ref, *, parts):
    rows = h_ref.shape[0] // parts
    sl = [slice(p * rows, (p + 1) * rows) for p in range(parts)]
    xn = [_rms(h_ref[s, :], pre_ref[...]).astype(BF16) for s in sl]
    acc = [jnp.zeros((rows, D_MODEL), F32) for _ in sl]
    off = 0
    for c in _ffn_chunks():
        gu = [(_dot(x, w1_ref[:, off:off + c]), _dot(x, w1_ref[:, FF_DIM + off:FF_DIM + off + c]))
              for x in xn]
        for p, (g, u) in enumerate(gu):
            a = (g * jax.nn.sigmoid(g) * u).astype(BF16)
            acc[p] = acc[p] + _dot(a, w2_ref[off:off + c, :])
        off += c
    for p, s in enumerate(sl):
        o_ref[s, :] = h_ref[s, :] + 0.5 * _rms(acc[p], post_ref[...])


def _ffn(h, pre_w, w1, w2, post_w, tm):
    t = h.shape[0]
    row = pl.BlockSpec((tm, D_MODEL), lambda i: (i, 0))
    return pl.pallas_call(
        functools.partial(_ffn_kernel, parts=FFN_PARTS),
        grid=(t // tm,),
        in_specs=[row, _const_spec((1, D_MODEL)), _const_spec((D_MODEL, 2 * FF_DIM)),
                  _const_spec((FF_DIM, D_MODEL)), _const_spec((1, D_MODEL))],
        out_specs=row,
        out_shape=jax.ShapeDtypeStruct((t, D_MODEL), F32),
        compiler_params=pltpu.CompilerParams(dimension_semantics=("parallel",),
                                             vmem_limit_bytes=VMEM_LIMIT),
        name="ffn",
    )(h, pre_w, w1, w2, post_w)


def _rope_lanes(x, c, s):
    return x * c + pltpu.roll(x, LANES - MLA_ROPE, 1) * s


def _tile4(x):
    y = x + pltpu.roll(x, 32, 1)
    return y + pltpu.roll(y, 64, 1)


def _pre_kernel(h_ref, pos_ref, rows_ref, pre_ref, win_ref, qn_ref, wuq_ref, kvn_ref, wuk_ref, wuvt_ref,
                q_ref, k_ref, vt_ref, rq_ref, rk_ref, rv_ref, *, tk):
    parts = h_ref.shape[0] // tk
    sl = [slice(p * tk, (p + 1) * tk) for p in range(parts)]
    rows = rows_ref[...]
    half = RET_QK // 2
    k_scale = RET_DK ** -0.5

    hn = [_rms(h_ref[s, :], pre_ref[...]).astype(BF16) for s in sl]
    lat = [_dot(x, win_ref[:, :PRE_RQ]) for x in hn]

    up = []
    for p in range(parts):
        cq = _rms(lat[p][:, PRE_CQ:PRE_CKV], qn_ref[...]).astype(BF16)
        ckv = _rms(lat[p][:, PRE_CKV:PRE_KR], kvn_ref[...]).astype(BF16)
        up.append((_dot(cq, wuq_ref[...]), _dot(ckv, wuk_ref[...]), _dot_nt(wuvt_ref[...], ckv)))

    ret = [_dot(x, win_ref[:, PRE_RQ:]) for x in hn]

    for p, s in enumerate(sl):
        q_raw, k_raw, vt = up[p]
        ang = pos_ref[p * tk // 2:(p + 1) * tk // 2, :] * (rows[0:1, :] + rows[4:5, :])
        cos, sin = jnp.cos(ang), jnp.sin(ang)
        tabs = []
        for m_mla, m_ret, shift in ((rows[2:3, :], rows[3:4, :], 32), (rows[5:6, :], rows[6:7, :], 96)):
            tabs.append((rows[1:2, :] + pltpu.roll(cos * m_mla, shift, 1),
                         pltpu.roll(sin * (m_mla * rows[7:8, :]), shift, 1),
                         _tile4(cos * m_ret), _tile4(sin * m_ret)))
        c_q, s_q, c_r, s_r = (jnp.concatenate([a, b], axis=0) for a, b in zip(*tabs))

        kr = _rope_lanes(lat[p][:, PRE_KR:PRE_RQ], c_q, s_q)
        for hd in range(MLA_HEADS):
            cols = slice(hd * HEAD_PAD, (hd + 1) * HEAD_PAD)
            q_ref[s, cols] = (_rope_lanes(q_raw[:, cols], c_q, s_q) * ATTN_LOG2_SCALE).astype(BF16)
            k_ref[s, cols] = (k_raw[:, cols] + kr).astype(BF16)
        row = lax.broadcasted_iota(jnp.int32, vt.shape, 0)
        vt = jnp.where(row % HEAD_PAD == MLA_V, 1.0, vt).astype(BF16)
        for hd in range(MLA_HEADS):
            vt_ref[0, hd, p] = vt[hd * HEAD_PAD:(hd + 1) * HEAD_PAD, :]

        rq1, rq2 = ret[p][:, :half], ret[p][:, half:RET_QK]
        rq_ref[s, :half] = (rq1 * c_r - rq2 * s_r).astype(BF16)
        rq_ref[s, half:] = (rq2 * c_r + rq1 * s_r).astype(BF16)
        rk1, rk2 = ret[p][:, RET_QK:RET_QK + half], ret[p][:, RET_QK + half:2 * RET_QK]
        rk_ref[s, :half] = (rk1 * c_r - rk2 * s_r) * k_scale
        rk_ref[s, half:] = (rk2 * c_r + rk1 * s_r) * k_scale
        rv_ref[s, :] = ret[p][:, 2 * RET_QK:].astype(BF16)


def _mixer_pre(h, pos, rows, pre_w, w_in_pre, qn_w, w_uq, kvn_w, w_uk, w_uvt, batch, seq, tm, tk):
    t = h.shape[0]
    ns = seq // tm

    def row(width):
        return pl.BlockSpec((tm, width), lambda i: (i, 0))

    qk_w = MLA_HEADS * HEAD_PAD
    vt_shape = (batch, MLA_HEADS, seq // tk, HEAD_PAD, tk)
    vt_spec = pl.BlockSpec((1, MLA_HEADS, tm // tk, HEAD_PAD, tk), lambda i: (i // ns, 0, i % ns, 0, 0))
    return pl.pallas_call(
        functools.partial(_pre_kernel, tk=tk),
        grid=(t // tm,),
        in_specs=[row(D_MODEL), pl.BlockSpec((tm // 2, LANES), lambda i: (i, 0)),
                  _const_spec((8, LANES)), _const_spec((1, D_MODEL)),
                  _const_spec((D_MODEL, PRE_WIDTH)), _const_spec((1, MLA_Q_RANK)),
                  _const_spec((MLA_Q_RANK, qk_w)), _const_spec((1, MLA_KV_RANK)),
                  _const_spec((MLA_KV_RANK, qk_w)), _const_spec((qk_w, MLA_KV_RANK))],
        out_specs=[row(qk_w), row(qk_w), vt_spec, row(RET_QK), row(RET_QK), row(RET_V)],
        out_shape=[jax.ShapeDtypeStruct((t, qk_w), BF16), jax.ShapeDtypeStruct((t, qk_w), BF16),
                   jax.ShapeDtypeStruct(vt_shape, BF16), jax.ShapeDtypeStruct((t, RET_QK), BF16),
                   jax.ShapeDtypeStruct((t, RET_QK), F32), jax.ShapeDtypeStruct((t, RET_V), BF16)],
        compiler_params=pltpu.CompilerParams(dimension_semantics=("parallel",),
                                             vmem_limit_bytes=VMEM_LIMIT),
        name="mixer_pre",
    )(h, pos, rows, pre_w, w_in_pre, qn_w, w_uq, kvn_w, w_uk, w_uvt)


def _attn_kernel(q_ref, k_ref, vt_ref, o_ref, sa_ref, sb_ref, *, tq, tk):
    qi = pl.program_id(2)
    assert tq == 2 * tk
    heads = q_ref.shape[1] // HEAD_PAD
    q = [q_ref[:, hd * HEAD_PAD:(hd + 1) * HEAD_PAD] for hd in range(heads)]

    def qk(j, s_ref, q_lo=0):
        start = pl.multiple_of(j * tk, tk)
        tile_max = []
        for hd in range(heads):
            s = _dot_nt(k_ref[pl.ds(start, tk), hd * HEAD_PAD:(hd + 1) * HEAD_PAD], q[hd][q_lo:, :])
            s_ref[hd, :, q_lo:] = s
            tile_max.append(jnp.max(s, axis=0, keepdims=True))
        return tuple(tile_max)

    def softmax_pv(j, s_ref, tile_max, stats, q_lo=0, masked=False):
        out = []
        for hd in range(heads):
            m_all, acc_all = stats[hd]
            m, acc = m_all[:, q_lo:], acc_all[:, q_lo:]
            s = s_ref[hd, :, q_lo:]
            if masked:
                keep = (lax.broadcasted_iota(jnp.int32, s.shape, 0)
                        <= lax.broadcasted_iota(jnp.int32, s.shape, 1))
                s = jnp.where(keep, s, -jnp.inf)
                s_max = jnp.max(s, axis=0, keepdims=True)
            else:
                s_max = tile_max[hd]
            m_new = jnp.maximum(m, s_max)
            alpha = jnp.exp2(m - m_new)
            p = jnp.exp2(s - m_new).astype(BF16)
            acc = alpha * acc + _dot(vt_ref[hd, j], p)
            if q_lo:
                m_new = jnp.concatenate([m_all[:, :q_lo], m_new], axis=1)
                acc = jnp.concatenate([acc_all[:, :q_lo], acc], axis=1)
            out.append((m_new, acc))
        return tuple(out)

    def step(jj, carry):
        max_a, stats = carry
        j = 2 * jj
        max_b = qk(j + 1, sb_ref)
        stats = softmax_pv(j, sa_ref, max_a, stats)
        max_a = qk(j + 2, sa_ref)
        return max_a, softmax_pv(j + 1, sb_ref, max_b, stats)

    stats = tuple((jnp.full((1, tq), -jnp.inf, F32), jnp.zeros((HEAD_PAD, tq), F32))
                  for _ in range(heads))
    _, stats = lax.fori_loop(0, qi, step, (qk(0, sa_ref), stats))
    qk(2 * qi + 1, sb_ref, q_lo=tk)
    stats = softmax_pv(2 * qi, sa_ref, None, stats, masked=True)
    stats = softmax_pv(2 * qi + 1, sb_ref, None, stats, q_lo=tk, masked=True)
    o_t = jnp.concatenate([acc[:MLA_V] / acc[MLA_V:MLA_V + 1] for _, acc in stats], axis=0)
    o_ref[...] = o_t.T.astype(o_ref.dtype)


def _attention(q, k, vt, batch, seq, tq, tk):
    t = q.shape[0]
    nq = seq // tq
    hg = ATTN_HEADS_PER_STEP
    return pl.pallas_call(
        functools.partial(_attn_kernel, tq=tq, tk=tk),
        grid=(batch, MLA_HEADS // hg, nq),
        in_specs=[pl.BlockSpec((tq, hg * HEAD_PAD), lambda b, p, i: (b * nq + i, p)),
                  pl.BlockSpec((seq, hg * HEAD_PAD), lambda b, p, i: (b, p)),
                  pl.BlockSpec((None, hg, seq // tk, HEAD_PAD, tk), lambda b, p, i: (b, p, 0, 0, 0))],
        out_specs=pl.BlockSpec((tq, hg * MLA_V), lambda b, p, i: (b * nq + i, p)),
        out_shape=jax.ShapeDtypeStruct((t, MLA_HEADS * MLA_V), BF16),
        compiler_params=pltpu.CompilerParams(
            dimension_semantics=("parallel", "parallel", "arbitrary"),
            vmem_limit_bytes=VMEM_LIMIT),
        scratch_shapes=[pltpu.VMEM((hg, tk, tq), F32), pltpu.VMEM((hg, tk, tq), F32)],
        name="attention",
    )(q, k, vt)


def _ret_kernel(rq_ref, rk_ref, rv_ref, dec_ref, xi_ref, zeta_ref, cd_ref, mask_ref, y_ref, state_ref,
                *, chunks):
    @pl.when(pl.program_id(1) == 0)
    def _():
        state_ref[...] = jnp.zeros(state_ref.shape, F32)

    blocks = [(c, hd) for c in range(chunks) for hd in range(RET_HEADS)]
    rows = [slice(c * RET_CHUNK, (c + 1) * RET_CHUNK) for c in range(chunks)]
    cols = [slice(hd * RET_DV, (hd + 1) * RET_DV) for hd in range(RET_HEADS)]
    q = [rq_ref[r, :] for r in rows]
    scores, kv = {}, {}
    for c, hd in blocks:
        km = rk_ref[rows[c], :] * mask_ref[hd]
        v = rv_ref[rows[c], cols[hd]]
        scores[c, hd] = _dot_nt(q[c], km.astype(BF16))
        kv[c, hd] = _dot_tn((km * zeta_ref[hd]).astype(BF16), v)
    prev = {}
    for hd in range(RET_HEADS):
        state = state_ref[hd]
        for c in range(chunks):
            prev[c, hd] = state
            state = state * cd_ref[hd] + kv[c, hd]
        state_ref[hd] = state
    for c, hd in blocks:
        inner = (scores[c, hd] * dec_ref[hd]).astype(BF16)
        y = (_dot(inner, rv_ref[rows[c], cols[hd]])
             + _dot(q[c], prev[c, hd].astype(BF16)) * xi_ref[hd])
        mu = jnp.mean(y, axis=-1, keepdims=True)
        yc = y - mu
        var = jnp.mean(yc * yc, axis=-1, keepdims=True)
        y_ref[rows[c], cols[hd]] = yc * lax.rsqrt(var + GN_EPS)


def _retention(rq, rk, rv, consts, batch, seq, tm):
    t = rq.shape[0]
    ns = seq // tm
    dec, xi, zeta, cd, mask = consts

    def row(width):
        return pl.BlockSpec((tm, width), lambda b, i: (b * ns + i, 0))

    return pl.pallas_call(
        functools.partial(_ret_kernel, chunks=tm // RET_CHUNK),
        grid=(batch, ns),
        in_specs=[row(RET_QK), row(RET_QK), row(RET_V), _const_spec(dec.shape), _const_spec(xi.shape),
                  _const_spec(zeta.shape), _const_spec(cd.shape), _const_spec(mask.shape)],
        out_specs=row(RET_V),
        out_shape=jax.ShapeDtypeStruct((t, RET_V), F32),
        scratch_shapes=[pltpu.VMEM((RET_HEADS, RET_QK, RET_DV), F32)],
        compiler_params=pltpu.CompilerParams(dimension_semantics=("parallel", "arbitrary"),
                                             vmem_limit_bytes=VMEM_LIMIT),
        name="retention",
    )(rq, rk, rv, dec, xi, zeta, cd, mask)


def _retention_consts():
    hh = jnp.arange(RET_HEADS, dtype=F32)
    log_gamma = jnp.log(1.0 - 2.0 ** (-5.0 - hh))
    idx = jnp.arange(RET_CHUNK, dtype=F32)
    diff = idx[:, None] - idx[None, :]
    dec = jnp.where(diff >= 0, jnp.exp(jnp.maximum(diff, 0.0) * log_gamma[:, None, None]), 0.0)
    zeta = jnp.exp((RET_CHUNK - 1 - idx) * log_gamma[:, None])
    xi = jnp.exp((idx + 1.0) * log_gamma[:, None])
    cd = jnp.exp(RET_CHUNK * log_gamma)
    xi_b = jnp.broadcast_to(xi[:, :, None], (RET_HEADS, RET_CHUNK, RET_DV))
    zeta_b = jnp.broadcast_to(zeta[:, :, None], (RET_HEADS, RET_CHUNK, RET_QK))
    cd_b = jnp.broadcast_to(cd[:, None, None], (RET_HEADS, 1, RET_DV))
    lane = jnp.arange(RET_QK)
    head_of_lane = (lane % (RET_QK // 2)) // (RET_DK // 2)
    mask = (head_of_lane[None, :] == jnp.arange(RET_HEADS)[:, None]).astype(F32)[:, None, :]
    return dec, xi_b, zeta_b, cd_b, mask


def _post_kernel(h_ref, o_ref, y_ref, pre_ref, wg_ref, gn_ref, wbm_ref, wbr_ref, wout_ref, post_ref,
                 out_ref, *, parts):
    rows = h_ref.shape[0] // parts
    sl = [slice(p * rows, (p + 1) * rows) for p in range(parts)]
    hn = [_rms(h_ref[s, :], pre_ref[...]).astype(BF16) for s in sl]
    rg = [_dot(x, wg_ref[:, :RET_V]) for x in hn]
    o_mla = [_dot(o_ref[s, :], wbm_ref[...]) for s in sl]
    gates = [_dot(x, wg_ref[:, RET_V:]) for x in hn]
    o_ret = []
    for p, s in enumerate(sl):
        a = (rg[p] * jax.nn.sigmoid(rg[p])) * (y_ref[s, :] * gn_ref[...])
        o_ret.append(_dot(a.astype(BF16), wbr_ref[...]))
    m = []
    for p in range(parts):
        merged = (jax.nn.sigmoid(gates[p][:, :D_MODEL]) * o_mla[p]
                  + jax.nn.sigmoid(gates[p][:, D_MODEL:]) * o_ret[p])
        m.append(_dot(merged.astype(BF16), wout_ref[...]))
    for p, s in enumerate(sl):
        out_ref[s, :] = h_ref[s, :] + _rms(m[p], post_ref[...])


def _mixer_post(h, o, y, pre_w, w_gates, gn_w, w_bm, w_br, w_out, post_w, tm):
    t = h.shape[0]

    def row(width):
        return pl.BlockSpec((tm, width), lambda i: (i, 0))

    return pl.pallas_call(
        functools.partial(_post_kernel, parts=POST_PARTS),
        grid=(t // tm,),
        in_specs=[row(D_MODEL), row(MLA_HEADS * MLA_V), row(RET_V), _const_spec((1, D_MODEL)),
                  _const_spec(w_gates.shape), _const_spec((1, RET_V)), _const_spec(w_bm.shape),
                  _const_spec(w_br.shape), _const_spec(w_out.shape), _const_spec((1, D_MODEL))],
        out_specs=row(D_MODEL),
        out_shape=jax.ShapeDtypeStruct((t, D_MODEL), F32),
        compiler_params=pltpu.CompilerParams(dimension_semantics=("parallel",),
                                             vmem_limit_bytes=VMEM_LIMIT),
        name="mixer_post",
    )(h, o, y, pre_w, w_gates, gn_w, w_bm, w_br, w_out, post_w)


def _rope_rows():
    lane = jnp.arange(LANES)
    half_m, half_r = MLA_ROPE // 2, RET_DK // 2
    f_mla = ROPE_BASE ** (-jnp.arange(half_m, dtype=F32) / half_m)
    f_ret = ROPE_BASE ** (-jnp.arange(half_r, dtype=F32) / half_r)
    token = jnp.concatenate([f_ret, f_mla, f_mla, jnp.zeros((LANES // 2,), F32)])
    in_tok = lane % (LANES // 2)
    m_ret = (in_tok < half_r).astype(F32)
    m_mla = (in_tok >= half_r).astype(F32)
    first = (lane < LANES // 2).astype(F32)
    sign = jnp.where(in_tok < half_r, 0.0, jnp.where(in_tok < half_r + half_m, -1.0, 1.0))
    rows = [token, (lane < MLA_NOPE).astype(F32), m_mla * first, m_ret * first,
            jnp.roll(token, LANES // 2), m_mla * (1 - first), m_ret * (1 - first), sign]
    return jnp.stack(rows)


def _split_halves(w, heads, dim):
    h = dim // 2
    return ([w[:, hd * dim:hd * dim + h] for hd in range(heads)]
            + [w[:, hd * dim + h:(hd + 1) * dim] for hd in range(heads)])


def _rope_cols(w):
    h = MLA_ROPE // 2
    return jnp.concatenate([w, w[..., h:], w[..., :h]], axis=-1)


def _layout_w_in(w_in):
    sizes = (MLA_Q_RANK, MLA_KV_RANK, MLA_ROPE, RET_QK, RET_QK, RET_V, RET_V, D_MODEL, D_MODEL)
    parts, off = [], 0
    for s in sizes:
        parts.append(w_in[:, off:off + s])
        off += s
    w_cq, w_ckv, w_kr, w_rq, w_rk, w_rv = parts[:6]
    rows = w_in.shape[0]
    w_pre = jnp.concatenate([w_cq, w_ckv, jnp.zeros((rows, MLA_NOPE), w_in.dtype), _rope_cols(w_kr)]
                            + _split_halves(w_rq, RET_HEADS, RET_DK)
                            + _split_halves(w_rk, RET_HEADS, RET_DK) + [w_rv], axis=1)
    w_gates = w_in[:, w_in.shape[1] - (RET_V + 2 * D_MODEL):]
    return w_pre.astype(BF16), w_gates.astype(BF16)


def _layout_w_uq(w_uq):
    w = w_uq.reshape(MLA_Q_RANK, MLA_HEADS, MLA_NOPE + MLA_ROPE)
    w = jnp.concatenate([w[..., :MLA_NOPE], _rope_cols(w[..., MLA_NOPE:])], axis=-1)
    return w.reshape(MLA_Q_RANK, MLA_HEADS * HEAD_PAD).astype(BF16)


def _layout_w_ukv(w_ukv):
    w = w_ukv.reshape(MLA_KV_RANK, MLA_HEADS, MLA_NOPE + MLA_V)
    w_k = jnp.pad(w[:, :, :MLA_NOPE], ((0, 0), (0, 0), (0, HEAD_PAD - MLA_NOPE)))
    w_v = jnp.pad(w[:, :, MLA_NOPE:], ((0, 0), (0, 0), (0, HEAD_PAD - MLA_V)))
    w_k = w_k.reshape(MLA_KV_RANK, -1).astype(BF16)
    w_vt = w_v.reshape(MLA_KV_RANK, -1).T.astype(BF16)
    return w_k, w_vt


def _pack_positions(positions, tk):
    b, s = positions.shape
    p = positions.reshape(b, s // tk, 2, tk // 2).transpose(0, 1, 3, 2).astype(F32)
    return jnp.repeat(p.reshape(b * s // 2, 2), LANES // 2, axis=1)


def _row(w):
    return w.reshape(1, -1)


def kernel(x, positions, ffn1_pre_w, ffn1_w1, ffn1_w2, ffn1_post_w, mix_pre_w, w_in, mla_q_norm_w,
           mla_w_uq, mla_kv_norm_w, mla_w_ukv, ret_gn_w, w_branch_mla, w_branch_ret, w_out, mix_post_w,
           ffn2_pre_w, ffn2_w1, ffn2_w2, ffn2_post_w):
    batch, seq, _ = x.shape
    depth = ffn1_w1.shape[0]
    t = batch * seq
    tm = min(512, seq)
    tm_ffn = min(1024, seq)
    tq = min(512, seq)
    tk = tq // 2
    h = x.reshape(t, D_MODEL)
    pos = _pack_positions(positions, tk)
    rows = _rope_rows()
    ret_consts = _retention_consts()
    for l in range(depth):
        h = _ffn(h, _row(ffn1_pre_w[l]), ffn1_w1[l].astype(BF16), ffn1_w2[l].astype(BF16),
                 _row(ffn1_post_w[l]), tm_ffn)
        w_pre, w_gates = _layout_w_in(w_in[l])
        w_uk, w_uvt = _layout_w_ukv(mla_w_ukv[l])
        q, k, vt, rq, rk, rv = _mixer_pre(h, pos, rows, _row(mix_pre_w[l]), w_pre, _row(mla_q_norm_w[l]),
                                          _layout_w_uq(mla_w_uq[l]), _row(mla_kv_norm_w[l]), w_uk, w_uvt,
                                          batch, seq, tm, tk)
        o = _attention(q, k, vt, batch, seq, tq, tk)
        y = _retention(rq, rk, rv, ret_consts, batch, seq, tm)
        h = _mixer_post(h, o, y, _row(mix_pre_w[l]), w_gates, _row(ret_gn_w[l]),
                        w_branch_mla[l].astype(BF16), w_branch_ret[l].astype(BF16), w_out[l].astype(BF16),
                        _row(mix_post_w[l]), tm)
        h = _ffn(h, _row(ffn2_pre_w[l]), ffn2_w1[l].astype(BF16), ffn2_w2[l].astype(BF16),
                 _row(ffn2_post_w[l]), tm_ffn)
    return h.reshape(batch, seq, D_MODEL)
```

```python
import functools
import math

import jax
import jax.numpy as jnp
from jax import lax
from jax.experimental import pallas as pl
from jax.experimental.pallas import tpu as pltpu

D_MODEL = 1024
MLA_HEADS = 8
MLA_NOPE = 64
MLA_ROPE = 32
MLA_V = 64
MLA_Q_RANK = 384
MLA_KV_RANK = 256
RET_HEADS = 4
RET_DK = 64
RET_DV = 128
RET_CHUNK = 128
FF_DIM = 2816
ROPE_BASE = 10000.0
NORM_EPS = 1e-6
GN_EPS = 1e-6

LANES = 128
HEAD_PAD = 128
RET_QK = RET_HEADS * RET_DK
RET_V = RET_HEADS * RET_DV
VMEM_LIMIT = 56 * 1024 * 1024
DENSE_TILE = 1024
RET_TILE = 1024
ATTN_Q_TILE = 512
FFN_PARTS = 4
POST_PARTS = 4
ATTN_HEADS_PER_STEP = 4

ATTN_LOG2_SCALE = math.log2(math.e) / math.sqrt(MLA_NOPE + MLA_ROPE)

BF16 = jnp.bfloat16
F32 = jnp.float32

PRE_CQ = 0
PRE_CKV = PRE_CQ + MLA_Q_RANK
PRE_KR = PRE_CKV + MLA_KV_RANK
PRE_RQ = PRE_KR + HEAD_PAD
PRE_RK = PRE_RQ + RET_QK
PRE_RV = PRE_RK + RET_QK
PRE_WIDTH = PRE_RV + RET_V


def _rms(x, w):
    return x * lax.rsqrt(jnp.mean(x * x, axis=-1, keepdims=True) + NORM_EPS) * w


def _dot(a, b):
    return jnp.dot(a, b, preferred_element_type=F32)


def _dot_nt(a, b):
    return lax.dot_general(a, b, (((1,), (1,)), ((), ())), preferred_element_type=F32)


def _dot_tn(a, b):
    return lax.dot_general(a, b, (((0,), (0,)), ((), ())), preferred_element_type=F32)


def _const_spec(shape):
    nd = len(shape)
    return pl.BlockSpec(shape, lambda *_: (0,) * nd, pipeline_mode=pl.Buffered(1))


def _ffn_kernel(h_ref, pre_ref, w1_ref, w2_ref, post_ref, o_ref, *, parts):
    rows = h_ref.shape[0] // parts
    sl = [slice(p * rows, (p + 1) * rows) for p in range(parts)]
    xn = [_rms(h_ref[s, :], pre_ref[...]).astype(BF16) for s in sl]
    gu = [_dot(x, w1_ref[...]) for x in xn]
    f = []
    for p in range(parts):
        g, u = gu[p][:, :FF_DIM], gu[p][:, FF_DIM:]
        f.append(_dot((g * jax.nn.sigmoid(g) * u).astype(BF16), w2_ref[...]))
    for p, s in enumerate(sl):
        o_ref[s, :] = h_ref[s, :] + 0.5 * _rms(f[p], post_ref[...])


def _ffn(h, pre_w, w1, w2, post_w, tm):
    t = h.shape[0]
    row = pl.BlockSpec((tm, D_MODEL), lambda i: (i, 0))
    return pl.pallas_call(
        functools.partial(_ffn_kernel, parts=FFN_PARTS),
        grid=(t // tm,),
        in_specs=[row, _const_spec((1, D_MODEL)), _const_spec((D_MODEL, 2 * FF_DIM)),
                  _const_spec((FF_DIM, D_MODEL)), _const_spec((1, D_MODEL))],
        out_specs=row,
        out_shape=jax.ShapeDtypeStruct((t, D_MODEL), F32),
        compiler_params=pltpu.CompilerParams(dimension_semantics=("parallel",),
                                             vmem_limit_bytes=VMEM_LIMIT),
        name="ffn",
    )(h, pre_w, w1, w2, post_w)


def _rope_lanes(x, c, s):
    return x * c + pltpu.roll(x, LANES - MLA_ROPE, 1) * s


def _tile4(x):
    y = x + pltpu.roll(x, 32, 1)
    return y + pltpu.roll(y, 64, 1)


def _pre_kernel(h_ref, pos_ref, rows_ref, pre_ref, win_ref, qn_ref, wuq_ref, kvn_ref, wuk_ref, wuvt_ref,
                q_ref, k_ref, vt_ref, rq_ref, rk_ref, rv_ref, *, tk):
    parts = h_ref.shape[0] // tk
    sl = [slice(p * tk, (p + 1) * tk) for p in range(parts)]
    rows = rows_ref[...]
    half = RET_QK // 2
    k_scale = RET_DK ** -0.5

    hn = [_rms(h_ref[s, :], pre_ref[...]).astype(BF16) for s in sl]
    lat = [_dot(x, win_ref[:, :PRE_RQ]) for x in hn]

    up = []
    for p in range(parts):
        cq = _rms(lat[p][:, PRE_CQ:PRE_CKV], qn_ref[...]).astype(BF16)
        ckv = _rms(lat[p][:, PRE_CKV:PRE_KR], kvn_ref[...]).astype(BF16)
        up.append((_dot(cq, wuq_ref[...]), _dot(ckv, wuk_ref[...]), _dot_nt(wuvt_ref[...], ckv)))

    ret = [_dot(x, win_ref[:, PRE_RQ:]) for x in hn]

    for p, s in enumerate(sl):
        q_raw, k_raw, vt = up[p]
        ang = pos_ref[p * tk // 2:(p + 1) * tk // 2, :] * (rows[0:1, :] + rows[4:5, :])
        cos, sin = jnp.cos(ang), jnp.sin(ang)
        tabs = []
        for m_mla, m_ret, shift in ((rows[2:3, :], rows[3:4, :], 32), (rows[5:6, :], rows[6:7, :], 96)):
            tabs.append((rows[1:2, :] + pltpu.roll(cos * m_mla, shift, 1),
                         pltpu.roll(sin * (m_mla * rows[7:8, :]), shift, 1),
                         _tile4(cos * m_ret), _tile4(sin * m_ret)))
        c_q, s_q, c_r, s_r = (jnp.concatenate([a, b], axis=0) for a, b in zip(*tabs))

        kr = _rope_lanes(lat[p][:, PRE_KR:PRE_RQ], c_q, s_q)
        for hd in range(MLA_HEADS):
            cols = slice(hd * HEAD_PAD, (hd + 1) * HEAD_PAD)
            q_ref[s, cols] = (_rope_lanes(q_raw[:, cols], c_q, s_q) * ATTN_LOG2_SCALE).astype(BF16)
            k_ref[s, cols] = (k_raw[:, cols] + kr).astype(BF16)
        row = lax.broadcasted_iota(jnp.int32, vt.shape, 0)
        vt = jnp.where(row % HEAD_PAD == MLA_V, 1.0, vt).astype(BF16)
        for hd in range(MLA_HEADS):
            vt_ref[0, hd, p] = vt[hd * HEAD_PAD:(hd + 1) * HEAD_PAD, :]

        rq1, rq2 = ret[p][:, :half], ret[p][:, half:RET_QK]
        rq_ref[s, :half] = (rq1 * c_r - rq2 * s_r).astype(BF16)
        rq_ref[s, half:] = (rq2 * c_r + rq1 * s_r).astype(BF16)
        rk1, rk2 = ret[p][:, RET_QK:RET_QK + half], ret[p][:, RET_QK + half:2 * RET_QK]
        rk_ref[s, :half] = (rk1 * c_r - rk2 * s_r) * k_scale
        rk_ref[s, half:] = (rk2 * c_r + rk1 * s_r) * k_scale
        rv_ref[s, :] = ret[p][:, 2 * RET_QK:].astype(BF16)


def _mixer_pre(h, pos, rows, pre_w, w_in_pre, qn_w, w_uq, kvn_w, w_uk, w_uvt, batch, seq, tm, tk):
    t = h.shape[0]
    ns = seq // tm

    def row(width):
        return pl.BlockSpec((tm, width), lambda i: (i, 0))

    qk_w = MLA_HEADS * HEAD_PAD
    vt_shape = (batch, MLA_HEADS, seq // tk, HEAD_PAD, tk)
    vt_spec = pl.BlockSpec((1, MLA_HEADS, tm // tk, HEAD_PAD, tk), lambda i: (i // ns, 0, i % ns, 0, 0))
    return pl.pallas_call(
        functools.partial(_pre_kernel, tk=tk),
        grid=(t // tm,),
        in_specs=[row(D_MODEL), pl.BlockSpec((tm // 2, LANES), lambda i: (i, 0)),
                  _const_spec((8, LANES)), _const_spec((1, D_MODEL)),
                  _const_spec((D_MODEL, PRE_WIDTH)), _const_spec((1, MLA_Q_RANK)),
                  _const_spec((MLA_Q_RANK, qk_w)), _const_spec((1, MLA_KV_RANK)),
                  _const_spec((MLA_KV_RANK, qk_w)), _const_spec((qk_w, MLA_KV_RANK))],
        out_specs=[row(qk_w), row(qk_w), vt_spec, row(RET_QK), row(RET_QK), row(RET_V)],
        out_shape=[jax.ShapeDtypeStruct((t, qk_w), BF16), jax.ShapeDtypeStruct((t, qk_w), BF16),
                   jax.ShapeDtypeStruct(vt_shape, BF16), jax.ShapeDtypeStruct((t, RET_QK), BF16),
                   jax.ShapeDtypeStruct((t, RET_QK), F32), jax.ShapeDtypeStruct((t, RET_V), BF16)],
        compiler_params=pltpu.CompilerParams(dimension_semantics=("parallel",),
                                             vmem_limit_bytes=VMEM_LIMIT),
        name="mixer_pre",
    )(h, pos, rows, pre_w, w_in_pre, qn_w, w_uq, kvn_w, w_uk, w_uvt)


def _attn_kernel(q_ref, k_ref, vt_ref, o_ref, sa_ref, sb_ref, *, tq, tk):
    qi = pl.program_id(2)
    assert tq == 2 * tk
    heads = q_ref.shape[1] // HEAD_PAD
    q = [q_ref[:, hd * HEAD_PAD:(hd + 1) * HEAD_PAD] for hd in range(heads)]

    def qk(j, s_ref, q_lo=0):
        start = pl.multiple_of(j * tk, tk)
        tile_max = []
        for hd in range(heads):
            s = _dot_nt(k_ref[pl.ds(start, tk), hd * HEAD_PAD:(hd + 1) * HEAD_PAD], q[hd][q_lo:, :])
            s_ref[hd, :, q_lo:] = s
            tile_max.append(jnp.max(s, axis=0, keepdims=True))
        return tuple(tile_max)

    def softmax_pv(j, s_ref, tile_max, stats, q_lo=0, masked=False):
        out = []
        for hd in range(heads):
            m_all, acc_all = stats[hd]
            m, acc = m_all[:, q_lo:], acc_all[:, q_lo:]
            s = s_ref[hd, :, q_lo:]
            if masked:
                keep = (lax.broadcasted_iota(jnp.int32, s.shape, 0)
                        <= lax.broadcasted_iota(jnp.int32, s.shape, 1))
                s = jnp.where(keep, s, -jnp.inf)
                s_max = jnp.max(s, axis=0, keepdims=True)
            else:
                s_max = tile_max[hd]
            m_new = jnp.maximum(m, s_max)
            alpha = jnp.exp2(m - m_new)
            p = jnp.exp2(s - m_new).astype(BF16)
            acc = alpha * acc + _dot(vt_ref[hd, j], p)
            if q_lo:
                m_new = jnp.concatenate([m_all[:, :q_lo], m_new], axis=1)
                acc = jnp.concatenate([acc_all[:, :q_lo], acc], axis=1)
            out.append((m_new, acc))
        return tuple(out)

    def step(jj, carry):
        max_a, stats = carry
        j = 2 * jj
        max_b = qk(j + 1, sb_ref)
        stats = softmax_pv(j, sa_ref, max_a, stats)
        max_a = qk(j + 2, sa_ref)
        return max_a, softmax_pv(j + 1, sb_ref, max_b, stats)

    stats = tuple((jnp.full((1, tq), -jnp.inf, F32), jnp.zeros((HEAD_PAD, tq), F32))
                  for _ in range(heads))
    _, stats = lax.fori_loop(0, qi, step, (qk(0, sa_ref), stats))
    qk(2 * qi + 1, sb_ref, q_lo=tk)
    stats = softmax_pv(2 * qi, sa_ref, None, stats, masked=True)
    stats = softmax_pv(2 * qi + 1, sb_ref, None, stats, q_lo=tk, masked=True)
    o_t = jnp.concatenate([acc[:MLA_V] / acc[MLA_V:MLA_V + 1] for _, acc in stats], axis=0)
    o_ref[...] = o_t.T.astype(o_ref.dtype)


def _attention(q, k, vt, batch, seq, tq, tk):
    t = q.shape[0]
    nq = seq // tq
    hg = ATTN_HEADS_PER_STEP
    return pl.pallas_call(
        functools.partial(_attn_kernel, tq=tq, tk=tk),
        grid=(batch, MLA_HEADS // hg, nq),
        in_specs=[pl.BlockSpec((tq, hg * HEAD_PAD), lambda b, p, i: (b * nq + i, p)),
                  pl.BlockSpec((seq, hg * HEAD_PAD), lambda b, p, i: (b, p)),
                  pl.BlockSpec((None, hg, seq // tk, HEAD_PAD, tk), lambda b, p, i: (b, p, 0, 0, 0))],
        out_specs=pl.BlockSpec((tq, hg * MLA_V), lambda b, p, i: (b * nq + i, p)),
        out_shape=jax.ShapeDtypeStruct((t, MLA_HEADS * MLA_V), BF16),
        compiler_params=pltpu.CompilerParams(
            dimension_semantics=("parallel", "parallel", "arbitrary"),
            vmem_limit_bytes=VMEM_LIMIT),
        scratch_shapes=[pltpu.VMEM((hg, tk, tq), F32), pltpu.VMEM((hg, tk, tq), F32)],
        name="attention",
    )(q, k, vt)


def _ret_kernel(rq_ref, rk_ref, rv_ref, dec_ref, xi_ref, zeta_ref, cd_ref, mask_ref, y_ref, state_ref,
                *, chunks):
    @pl.when(pl.program_id(1) == 0)
    def _():
        state_ref[...] = jnp.zeros(state_ref.shape, F32)

    blocks = [(c, hd) for c in range(chunks) for hd in range(RET_HEADS)]
    rows = [slice(c * RET_CHUNK, (c + 1) * RET_CHUNK) for c in range(chunks)]
    cols = [slice(hd * RET_DV, (hd + 1) * RET_DV) for hd in range(RET_HEADS)]
    q = [rq_ref[r, :] for r in rows]
    scores, kv = {}, {}
    for c, hd in blocks:
        km = rk_ref[rows[c], :] * mask_ref[hd]
        v = rv_ref[rows[c], cols[hd]]
        scores[c, hd] = _dot_nt(q[c], km.astype(BF16))
        kv[c, hd] = _dot_tn((km * zeta_ref[hd]).astype(BF16), v)
    prev = {}
    for hd in range(RET_HEADS):
        state = state_ref[hd]
        for c in range(chunks):
            prev[c, hd] = state
            state = state * cd_ref[hd] + kv[c, hd]
        state_ref[hd] = state
    for c, hd in blocks:
        inner = (scores[c, hd] * dec_ref[hd]).astype(BF16)
        y = (_dot(inner, rv_ref[rows[c], cols[hd]])
             + _dot(q[c], prev[c, hd].astype(BF16)) * xi_ref[hd])
        mu = jnp.mean(y, axis=-1, keepdims=True)
        yc = y - mu
        var = jnp.mean(yc * yc, axis=-1, keepdims=True)
        y_ref[rows[c], cols[hd]] = yc * lax.rsqrt(var + GN_EPS)


def _retention(rq, rk, rv, consts, batch, seq, tm):
    t = rq.shape[0]
    ns = seq // tm
    dec, xi, zeta, cd, mask = consts

    def row(width):
        return pl.BlockSpec((tm, width), lambda b, i: (b * ns + i, 0))

    return pl.pallas_call(
        functools.partial(_ret_kernel, chunks=tm // RET_CHUNK),
        grid=(batch, ns),
        in_specs=[row(RET_QK), row(RET_QK), row(RET_V), _const_spec(dec.shape), _const_spec(xi.shape),
                  _const_spec(zeta.shape), _const_spec(cd.shape), _const_spec(mask.shape)],
        out_specs=row(RET_V),
        out_shape=jax.ShapeDtypeStruct((t, RET_V), F32),
        scratch_shapes=[pltpu.VMEM((RET_HEADS, RET_QK, RET_DV), F32)],
        compiler_params=pltpu.CompilerParams(dimension_semantics=("parallel", "arbitrary"),
                                             vmem_limit_bytes=VMEM_LIMIT),
        name="retention",
    )(rq, rk, rv, dec, xi, zeta, cd, mask)


def _retention_consts():
    hh = jnp.arange(RET_HEADS, dtype=F32)
    log_gamma = jnp.log(1.0 - 2.0 ** (-5.0 - hh))
    idx = jnp.arange(RET_CHUNK, dtype=F32)
    diff = idx[:, None] - idx[None, :]
    dec = jnp.where(diff >= 0, jnp.exp(jnp.maximum(diff, 0.0) * log_gamma[:, None, None]), 0.0)
    zeta = jnp.exp((RET_CHUNK - 1 - idx) * log_gamma[:, None])
    xi = jnp.exp((idx + 1.0) * log_gamma[:, None])
    cd = jnp.exp(RET_CHUNK * log_gamma)
    xi_b = jnp.broadcast_to(xi[:, :, None], (RET_HEADS, RET_CHUNK, RET_DV))
    zeta_b = jnp.broadcast_to(zeta[:, :, None], (RET_HEADS, RET_CHUNK, RET_QK))
    cd_b = jnp.broadcast_to(cd[:, None, None], (RET_HEADS, 1, RET_DV))
    lane = jnp.arange(RET_QK)
    head_of_lane = (lane % (RET_QK // 2)) // (RET_DK // 2)
    mask = (head_of_lane[None, :] == jnp.arange(RET_HEADS)[:, None]).astype(F32)[:, None, :]
    return dec, xi_b, zeta_b, cd_b, mask


def _post_kernel(h_ref, o_ref, y_ref, pre_ref, wg_ref, gn_ref, wbm_ref, wbr_ref, wout_ref, post_ref,
                 out_ref, *, parts):
    rows = h_ref.shape[0] // parts
    sl = [slice(p * rows, (p + 1) * rows) for p in range(parts)]
    hn = [_rms(h_ref[s, :], pre_ref[...]).astype(BF16) for s in sl]
    rg = [_dot(x, wg_ref[:, :RET_V]) for x in hn]
    o_mla = [_dot(o_ref[s, :], wbm_ref[...]) for s in sl]
    gates = [_dot(x, wg_ref[:, RET_V:]) for x in hn]
    o_ret = []
    for p, s in enumerate(sl):
        a = (rg[p] * jax.nn.sigmoid(rg[p])) * (y_ref[s, :] * gn_ref[...])
        o_ret.append(_dot(a.astype(BF16), wbr_ref[...]))
    m = []
    for p in range(parts):
        merged = (jax.nn.sigmoid(gates[p][:, :D_MODEL]) * o_mla[p]
                  + jax.nn.sigmoid(gates[p][:, D_MODEL:]) * o_ret[p])
        m.append(_dot(merged.astype(BF16), wout_ref[...]))
    for p, s in enumerate(sl):
        out_ref[s, :] = h_ref[s, :] + _rms(m[p], post_ref[...])


def _mixer_post(h, o, y, pre_w, w_gates, gn_w, w_bm, w_br, w_out, post_w, tm):
    t = h.shape[0]

    def row(width):
        return pl.BlockSpec((tm, width), lambda i: (i, 0))

    return pl.pallas_call(
        functools.partial(_post_kernel, parts=POST_PARTS),
        grid=(t // tm,),
        in_specs=[row(D_MODEL), row(MLA_HEADS * MLA_V), row(RET_V), _const_spec((1, D_MODEL)),
                  _const_spec(w_gates.shape), _const_spec((1, RET_V)), _const_spec(w_bm.shape),
                  _const_spec(w_br.shape), _const_spec(w_out.shape), _const_spec((1, D_MODEL))],
        out_specs=row(D_MODEL),
        out_shape=jax.ShapeDtypeStruct((t, D_MODEL), F32),
        compiler_params=pltpu.CompilerParams(dimension_semantics=("parallel",),
                                             vmem_limit_bytes=VMEM_LIMIT),
        name="mixer_post",
    )(h, o, y, pre_w, w_gates, gn_w, w_bm, w_br, w_out, post_w)


def _rope_rows():
    lane = jnp.arange(LANES)
    half_m, half_r = MLA_ROPE // 2, RET_DK // 2
    f_mla = ROPE_BASE ** (-jnp.arange(half_m, dtype=F32) / half_m)
    f_ret = ROPE_BASE ** (-jnp.arange(half_r, dtype=F32) / half_r)
    token = jnp.concatenate([f_ret, f_mla, f_mla, jnp.zeros((LANES // 2,), F32)])
    in_tok = lane % (LANES // 2)
    m_ret = (in_tok < half_r).astype(F32)
    m_mla = (in_tok >= half_r).astype(F32)
    first = (lane < LANES // 2).astype(F32)
    sign = jnp.where(in_tok < half_r, 0.0, jnp.where(in_tok < half_r + half_m, -1.0, 1.0))
    rows = [token, (lane < MLA_NOPE).astype(F32), m_mla * first, m_ret * first,
            jnp.roll(token, LANES // 2), m_mla * (1 - first), m_ret * (1 - first), sign]
    return jnp.stack(rows)


def _split_halves(w, heads, dim):
    h = dim // 2
    return ([w[:, hd * dim:hd * dim + h] for hd in range(heads)]
            + [w[:, hd * dim + h:(hd + 1) * dim] for hd in range(heads)])


def _rope_cols(w):
    h = MLA_ROPE // 2
    return jnp.concatenate([w, w[..., h:], w[..., :h]], axis=-1)


def _layout_w_in(w_in):
    sizes = (MLA_Q_RANK, MLA_KV_RANK, MLA_ROPE, RET_QK, RET_QK, RET_V, RET_V, D_MODEL, D_MODEL)
    parts, off = [], 0
    for s in sizes:
        parts.append(w_in[:, off:off + s])
        off += s
    w_cq, w_ckv, w_kr, w_rq, w_rk, w_rv = parts[:6]
    rows = w_in.shape[0]
    w_pre = jnp.concatenate([w_cq, w_ckv, jnp.zeros((rows, MLA_NOPE), w_in.dtype), _rope_cols(w_kr)]
                            + _split_halves(w_rq, RET_HEADS, RET_DK)
                            + _split_halves(w_rk, RET_HEADS, RET_DK) + [w_rv], axis=1)
    w_gates = w_in[:, w_in.shape[1] - (RET_V + 2 * D_MODEL):]
    return w_pre.astype(BF16), w_gates.astype(BF16)


def _layout_w_uq(w_uq):
    w = w_uq.reshape(MLA_Q_RANK, MLA_HEADS, MLA_NOPE + MLA_ROPE)
    w = jnp.concatenate([w[..., :MLA_NOPE], _rope_cols(w[..., MLA_NOPE:])], axis=-1)
    return w.reshape(MLA_Q_RANK, MLA_HEADS * HEAD_PAD).astype(BF16)


def _layout_w_ukv(w_ukv):
    w = w_ukv.reshape(MLA_KV_RANK, MLA_HEADS, MLA_NOPE + MLA_V)
    w_k = jnp.pad(w[:, :, :MLA_NOPE], ((0, 0), (0, 0), (0, HEAD_PAD - MLA_NOPE)))
    w_v = jnp.pad(w[:, :, MLA_NOPE:], ((0, 0), (0, 0), (0, HEAD_PAD - MLA_V)))
    w_k = w_k.reshape(MLA_KV_RANK, -1).astype(BF16)
    w_vt = w_v.reshape(MLA_KV_RANK, -1).T.astype(BF16)
    return w_k, w_vt


def _pack_positions(positions, tk):
    b, s = positions.shape
    p = positions.reshape(b * s // tk, 2, tk // 2).astype(F32)
    first = p[:, 0, :].reshape(b * s // 2, 1)
    second = p[:, 1, :].reshape(b * s // 2, 1)
    lane = lax.broadcasted_iota(jnp.int32, (b * s // 2, LANES), 1)
    return jnp.where(lane < LANES // 2, first, second)


def _row(w):
    return w.reshape(1, -1)


def kernel(x, positions, ffn1_pre_w, ffn1_w1, ffn1_w2, ffn1_post_w, mix_pre_w, w_in, mla_q_norm_w,
           mla_w_uq, mla_kv_norm_w, mla_w_ukv, ret_gn_w, w_branch_mla, w_branch_ret, w_out, mix_post_w,
           ffn2_pre_w, ffn2_w1, ffn2_w2, ffn2_post_w):
    batch, seq, _ = x.shape
    depth = ffn1_w1.shape[0]
    t = batch * seq
    tm = min(RET_TILE, seq)
    tm_ffn = min(DENSE_TILE, seq)
    tq = min(ATTN_Q_TILE, seq)
    tk = tq // 2
    h = x.reshape(t, D_MODEL)
    pos = _pack_positions(positions, tk)
    rows = _rope_rows()
    ret_consts = _retention_consts()
    for l in range(depth):
        h = _ffn(h, _row(ffn1_pre_w[l]), ffn1_w1[l].astype(BF16), ffn1_w2[l].astype(BF16),
                 _row(ffn1_post_w[l]), tm_ffn)
        w_pre, w_gates = _layout_w_in(w_in[l])
        w_uk, w_uvt = _layout_w_ukv(mla_w_ukv[l])
        q, k, vt, rq, rk, rv = _mixer_pre(h, pos, rows, _row(mix_pre_w[l]), w_pre, _row(mla_q_norm_w[l]),
                                          _layout_w_uq(mla_w_uq[l]), _row(mla_kv_norm_w[l]), w_uk, w_uvt,
                                          batch, seq, tm_ffn, tk)
        o = _attention(q, k, vt, batch, seq, tq, tk)
        y = _retention(rq, rk, rv, ret_consts, batch, seq, tm)
        h = _mixer_post(h, o, y, _row(mix_pre_w[l]), w_gates, _row(ret_gn_w[l]),
                        w_branch_mla[l].astype(BF16), w_branch_ret[l].astype(BF16), w_out[l].astype(BF16),
                        _row(mix_post_w[l]), tm_ffn)
        h = _ffn(h, _row(ffn2_pre_w[l]), ffn2_w1[l].astype(BF16), ffn2_w2[l].astype(BF16),
                 _row(ffn2_post_w[l]), tm_ffn)
    return h.reshape(batch, seq, D_MODEL)
```

```python
import functools
import math

import jax
import jax.numpy as jnp
from jax import lax
from jax.experimental import pallas as pl
from jax.experimental.pallas import tpu as pltpu

D_MODEL = 1024
MLA_HEADS = 8
MLA_NOPE = 64
MLA_ROPE = 32
MLA_V = 64
MLA_Q_RANK = 384
MLA_KV_RANK = 256
RET_HEADS = 4
RET_DK = 64
RET_DV = 128
RET_CHUNK = 128
FF_DIM = 2816
ROPE_BASE = 10000.0
NORM_EPS = 1e-6
GN_EPS = 1e-6

LANES = 128
HEAD_PAD = 128
RET_QK = RET_HEADS * RET_DK
RET_V = RET_HEADS * RET_DV
VMEM_LIMIT = 56 * 1024 * 1024
DENSE_TILE = 1024
RET_TILE = 1024
ATTN_Q_TILE = 1024
ATTN_KEY_BLOCK = 256
FFN_PARTS = 4
POST_PARTS = 4
ATTN_HEADS_PER_STEP = 4

ATTN_LOG2_SCALE = math.log2(math.e) / math.sqrt(MLA_NOPE + MLA_ROPE)

BF16 = jnp.bfloat16
F32 = jnp.float32

PRE_CQ = 0
PRE_CKV = PRE_CQ + MLA_Q_RANK
PRE_KR = PRE_CKV + MLA_KV_RANK
PRE_RQ = PRE_KR + HEAD_PAD
PRE_RK = PRE_RQ + RET_QK
PRE_RV = PRE_RK + RET_QK
PRE_WIDTH = PRE_RV + RET_V


def _rms(x, w):
    return x * lax.rsqrt(jnp.mean(x * x, axis=-1, keepdims=True) + NORM_EPS) * w


def _dot(a, b):
    return jnp.dot(a, b, preferred_element_type=F32)


def _dot_nt(a, b):
    return lax.dot_general(a, b, (((1,), (1,)), ((), ())), preferred_element_type=F32)


def _dot_tn(a, b):
    return lax.dot_general(a, b, (((0,), (0,)), ((), ())), preferred_element_type=F32)


def _const_spec(shape):
    nd = len(shape)
    return pl.BlockSpec(shape, lambda *_: (0,) * nd, pipeline_mode=pl.Buffered(1))


def _ffn_kernel(h_ref, pre_ref, w1_ref, w2_ref, post_ref, o_ref, *, parts):
    rows = h_ref.shape[0] // parts
    sl = [slice(p * rows, (p + 1) * rows) for p in range(parts)]
    xn = [_rms(h_ref[s, :], pre_ref[...]).astype(BF16) for s in sl]
    gu = [_dot(x, w1_ref[...]) for x in xn]
    f = []
    for p in range(parts):
        g, u = gu[p][:, :FF_DIM], gu[p][:, FF_DIM:]
        f.append(_dot((g * jax.nn.sigmoid(g) * u).astype(BF16), w2_ref[...]))
    for p, s in enumerate(sl):
        o_ref[s, :] = h_ref[s, :] + 0.5 * _rms(f[p], post_ref[...])


def _ffn(h, pre_w, w1, w2, post_w, tm):
    t = h.shape[0]
    row = pl.BlockSpec((tm, D_MODEL), lambda i: (i, 0))
    return pl.pallas_call(
        functools.partial(_ffn_kernel, parts=FFN_PARTS),
        grid=(t // tm,),
        in_specs=[row, _const_spec((1, D_MODEL)), _const_spec((D_MODEL, 2 * FF_DIM)),
                  _const_spec((FF_DIM, D_MODEL)), _const_spec((1, D_MODEL))],
        out_specs=row,
        out_shape=jax.ShapeDtypeStruct((t, D_MODEL), F32),
        compiler_params=pltpu.CompilerParams(dimension_semantics=("parallel",),
                                             vmem_limit_bytes=VMEM_LIMIT),
        name="ffn",
    )(h, pre_w, w1, w2, post_w)


def _rope_lanes(x, c, s):
    return x * c + pltpu.roll(x, LANES - MLA_ROPE, 1) * s


def _tile4(x):
    y = x + pltpu.roll(x, 32, 1)
    return y + pltpu.roll(y, 64, 1)


def _pre_kernel(h_ref, pos_ref, rows_ref, pre_ref, win_ref, qn_ref, wuq_ref, kvn_ref, wuk_ref, wuvt_ref,
                q_ref, k_ref, vt_ref, rq_ref, rk_ref, rv_ref, *, tk):
    parts = h_ref.shape[0] // tk
    sl = [slice(p * tk, (p + 1) * tk) for p in range(parts)]
    rows = rows_ref[...]
    half = RET_QK // 2
    k_scale = RET_DK ** -0.5

    hn = [_rms(h_ref[s, :], pre_ref[...]).astype(BF16) for s in sl]
    lat = [_dot(x, win_ref[:, :PRE_RQ]) for x in hn]

    up = []
    for p in range(parts):
        cq = _rms(lat[p][:, PRE_CQ:PRE_CKV], qn_ref[...]).astype(BF16)
        ckv = _rms(lat[p][:, PRE_CKV:PRE_KR], kvn_ref[...]).astype(BF16)
        up.append((_dot(cq, wuq_ref[...]), _dot(ckv, wuk_ref[...]), _dot_nt(wuvt_ref[...], ckv)))

    ret = [_dot(x, win_ref[:, PRE_RQ:]) for x in hn]

    for p, s in enumerate(sl):
        q_raw, k_raw, vt = up[p]
        ang = pos_ref[p * tk // 2:(p + 1) * tk // 2, :] * (rows[0:1, :] + rows[4:5, :])
        cos, sin = jnp.cos(ang), jnp.sin(ang)
        tabs = []
        for m_mla, m_ret, shift in ((rows[2:3, :], rows[3:4, :], 32), (rows[5:6, :], rows[6:7, :], 96)):
            tabs.append((rows[1:2, :] + pltpu.roll(cos * m_mla, shift, 1),
                         pltpu.roll(sin * (m_mla * rows[7:8, :]), shift, 1),
                         _tile4(cos * m_ret), _tile4(sin * m_ret)))
        c_q, s_q, c_r, s_r = (jnp.concatenate([a, b], axis=0) for a, b in zip(*tabs))

        kr = _rope_lanes(lat[p][:, PRE_KR:PRE_RQ], c_q, s_q)
        for hd in range(MLA_HEADS):
            cols = slice(hd * HEAD_PAD, (hd + 1) * HEAD_PAD)
            q_ref[s, cols] = (_rope_lanes(q_raw[:, cols], c_q, s_q) * ATTN_LOG2_SCALE).astype(BF16)
            k_ref[s, cols] = (k_raw[:, cols] + kr).astype(BF16)
        row = lax.broadcasted_iota(jnp.int32, vt.shape, 0)
        vt = jnp.where(row % HEAD_PAD == MLA_V, 1.0, vt).astype(BF16)
        for hd in range(MLA_HEADS):
            vt_ref[0, hd, p] = vt[hd * HEAD_PAD:(hd + 1) * HEAD_PAD, :]

        rq1, rq2 = ret[p][:, :half], ret[p][:, half:RET_QK]
        rq_ref[s, :half] = (rq1 * c_r - rq2 * s_r).astype(BF16)
        rq_ref[s, half:] = (rq2 * c_r + rq1 * s_r).astype(BF16)
        rk1, rk2 = ret[p][:, RET_QK:RET_QK + half], ret[p][:, RET_QK + half:2 * RET_QK]
        rk_ref[s, :half] = (rk1 * c_r - rk2 * s_r) * k_scale
        rk_ref[s, half:] = (rk2 * c_r + rk1 * s_r) * k_scale
        rv_ref[s, :] = ret[p][:, 2 * RET_QK:].astype(BF16)


def _mixer_pre(h, pos, rows, pre_w, w_in_pre, qn_w, w_uq, kvn_w, w_uk, w_uvt, batch, seq, tm, tk):
    t = h.shape[0]
    ns = seq // tm

    def row(width):
        return pl.BlockSpec((tm, width), lambda i: (i, 0))

    qk_w = MLA_HEADS * HEAD_PAD
    vt_shape = (batch, MLA_HEADS, seq // tk, HEAD_PAD, tk)
    vt_spec = pl.BlockSpec((1, MLA_HEADS, tm // tk, HEAD_PAD, tk), lambda i: (i // ns, 0, i % ns, 0, 0))
    return pl.pallas_call(
        functools.partial(_pre_kernel, tk=tk),
        grid=(t // tm,),
        in_specs=[row(D_MODEL), pl.BlockSpec((tm // 2, LANES), lambda i: (i, 0)),
                  _const_spec((8, LANES)), _const_spec((1, D_MODEL)),
                  _const_spec((D_MODEL, PRE_WIDTH)), _const_spec((1, MLA_Q_RANK)),
                  _const_spec((MLA_Q_RANK, qk_w)), _const_spec((1, MLA_KV_RANK)),
                  _const_spec((MLA_KV_RANK, qk_w)), _const_spec((qk_w, MLA_KV_RANK))],
        out_specs=[row(qk_w), row(qk_w), vt_spec, row(RET_QK), row(RET_QK), row(RET_V)],
        out_shape=[jax.ShapeDtypeStruct((t, qk_w), BF16), jax.ShapeDtypeStruct((t, qk_w), BF16),
                   jax.ShapeDtypeStruct(vt_shape, BF16), jax.ShapeDtypeStruct((t, RET_QK), BF16),
                   jax.ShapeDtypeStruct((t, RET_QK), F32), jax.ShapeDtypeStruct((t, RET_V), BF16)],
        compiler_params=pltpu.CompilerParams(dimension_semantics=("parallel",),
                                             vmem_limit_bytes=VMEM_LIMIT),
        name="mixer_pre",
    )(h, pos, rows, pre_w, w_in_pre, qn_w, w_uq, kvn_w, w_uk, w_uvt)


def _attn_kernel(q_ref, k_ref, vt_ref, o_ref, sa_ref, sb_ref, m_ref, acc_ref, *, tq, kb):
    qi = pl.program_id(2)
    heads = q_ref.shape[1] // HEAD_PAD
    nb = tq // (2 * kb)

    def qk(blk, n, s_ref, row0=0, q_lo=0):
        start = pl.multiple_of(blk * kb, kb)
        tile_max = []
        for hd in range(heads):
            cols = slice(hd * HEAD_PAD, (hd + 1) * HEAD_PAD)
            s = _dot_nt(k_ref[pl.ds(start, n * kb), cols], q_ref[q_lo:, cols])
            s_ref[hd, row0:row0 + n * kb, q_lo:] = s
            tile_max.append(jnp.max(s, axis=0, keepdims=True))
        return tuple(tile_max)

    def softmax_pv(blk, n, s_ref, tile_max, row0=0, q_lo=0, masked=False):
        for hd in range(heads):
            s = s_ref[hd, row0:row0 + n * kb, q_lo:]
            if masked:
                keep = (lax.broadcasted_iota(jnp.int32, s.shape, 0)
                        <= lax.broadcasted_iota(jnp.int32, s.shape, 1))
                s = jnp.where(keep, s, -jnp.inf)
                s_max = jnp.max(s, axis=0, keepdims=True)
            else:
                s_max = tile_max[hd]
            m = m_ref[hd, :, q_lo:]
            m_new = jnp.maximum(m, s_max)
            m_ref[hd, :, q_lo:] = m_new
            p = jnp.exp2(s - m_new).astype(BF16)
            vt = jnp.concatenate([vt_ref[hd, blk + i] for i in range(n)], axis=1)
            acc_ref[hd, :, q_lo:] = (jnp.exp2(m - m_new) * acc_ref[hd, :, q_lo:]
                                     + _dot(vt, p))

    def step(jj, max_a):
        blk = 2 * nb * jj
        max_b = qk(blk + nb, nb, sb_ref)
        softmax_pv(blk, nb, sa_ref, max_a)
        max_a = qk(blk + 2 * nb, nb, sa_ref)
        softmax_pv(blk + nb, nb, sb_ref, max_b)
        return max_a

    m_ref[...] = jnp.full(m_ref.shape, -jnp.inf, F32)
    acc_ref[...] = jnp.zeros(acc_ref.shape, F32)
    lax.fori_loop(0, qi, step, qk(0, nb, sa_ref))
    blk = 2 * nb * qi
    qk(blk + nb, nb, sb_ref, q_lo=nb * kb)
    softmax_pv(blk, nb, sa_ref, None, masked=True)
    softmax_pv(blk + nb, nb, sb_ref, None, q_lo=nb * kb, masked=True)
    o_t = jnp.concatenate([acc_ref[hd, :MLA_V, :] / acc_ref[hd, MLA_V:MLA_V + 1, :] for hd in range(heads)],
                          axis=0)
    o_ref[...] = o_t.T.astype(o_ref.dtype)


def _attention(q, k, vt, batch, seq, tq, kb):
    t = q.shape[0]
    nq = seq // tq
    hg = ATTN_HEADS_PER_STEP
    return pl.pallas_call(
        functools.partial(_attn_kernel, tq=tq, kb=kb),
        grid=(batch, MLA_HEADS // hg, nq),
        in_specs=[pl.BlockSpec((tq, hg * HEAD_PAD), lambda b, p, i: (b * nq + i, p)),
                  pl.BlockSpec((seq, hg * HEAD_PAD), lambda b, p, i: (b, p)),
                  pl.BlockSpec((None, hg, seq // kb, HEAD_PAD, kb), lambda b, p, i: (b, p, 0, 0, 0))],
        out_specs=pl.BlockSpec((tq, hg * MLA_V), lambda b, p, i: (b * nq + i, p)),
        out_shape=jax.ShapeDtypeStruct((t, MLA_HEADS * MLA_V), BF16),
        compiler_params=pltpu.CompilerParams(
            dimension_semantics=("parallel", "parallel", "arbitrary"),
            vmem_limit_bytes=VMEM_LIMIT),
        scratch_shapes=[pltpu.VMEM((hg, tq // 2, tq), F32), pltpu.VMEM((hg, tq // 2, tq), F32),
                        pltpu.VMEM((hg, 1, tq), F32), pltpu.VMEM((hg, HEAD_PAD, tq), F32)],
        name="attention",
    )(q, k, vt)


def _ret_kernel(rq_ref, rk_ref, rv_ref, dec_ref, xi_ref, zeta_ref, cd_ref, mask_ref, y_ref, state_ref,
                *, chunks):
    @pl.when(pl.program_id(1) == 0)
    def _():
        state_ref[...] = jnp.zeros(state_ref.shape, F32)

    blocks = [(c, hd) for c in range(chunks) for hd in range(RET_HEADS)]
    rows = [slice(c * RET_CHUNK, (c + 1) * RET_CHUNK) for c in range(chunks)]
    cols = [slice(hd * RET_DV, (hd + 1) * RET_DV) for hd in range(RET_HEADS)]
    q = [rq_ref[r, :] for r in rows]
    scores, kv = {}, {}
    for c, hd in blocks:
        km = rk_ref[rows[c], :] * mask_ref[hd]
        v = rv_ref[rows[c], cols[hd]]
        scores[c, hd] = _dot_nt(q[c], km.astype(BF16))
        kv[c, hd] = _dot_tn((km * zeta_ref[hd]).astype(BF16), v)
    prev = {}
    for hd in range(RET_HEADS):
        state = state_ref[hd]
        for c in range(chunks):
            prev[c, hd] = state
            state = state * cd_ref[hd] + kv[c, hd]
        state_ref[hd] = state
    for c, hd in blocks:
        inner = (scores[c, hd] * dec_ref[hd]).astype(BF16)
        y = (_dot(inner, rv_ref[rows[c], cols[hd]])
             + _dot(q[c], prev[c, hd].astype(BF16)) * xi_ref[hd])
        mu = jnp.mean(y, axis=-1, keepdims=True)
        yc = y - mu
        var = jnp.mean(yc * yc, axis=-1, keepdims=True)
        y_ref[rows[c], cols[hd]] = yc * lax.rsqrt(var + GN_EPS)


def _retention(rq, rk, rv, consts, batch, seq, tm):
    t = rq.shape[0]
    ns = seq // tm
    dec, xi, zeta, cd, mask = consts

    def row(width):
        return pl.BlockSpec((tm, width), lambda b, i: (b * ns + i, 0))

    return pl.pallas_call(
        functools.partial(_ret_kernel, chunks=tm // RET_CHUNK),
        grid=(batch, ns),
        in_specs=[row(RET_QK), row(RET_QK), row(RET_V), _const_spec(dec.shape), _const_spec(xi.shape),
                  _const_spec(zeta.shape), _const_spec(cd.shape), _const_spec(mask.shape)],
        out_specs=row(RET_V),
        out_shape=jax.ShapeDtypeStruct((t, RET_V), F32),
        scratch_shapes=[pltpu.VMEM((RET_HEADS, RET_QK, RET_DV), F32)],
        compiler_params=pltpu.CompilerParams(dimension_semantics=("parallel", "arbitrary"),
                                             vmem_limit_bytes=VMEM_LIMIT),
        name="retention",
    )(rq, rk, rv, dec, xi, zeta, cd, mask)


def _retention_consts():
    hh = jnp.arange(RET_HEADS, dtype=F32)
    log_gamma = jnp.log(1.0 - 2.0 ** (-5.0 - hh))
    idx = jnp.arange(RET_CHUNK, dtype=F32)
    diff = idx[:, None] - idx[None, :]
    dec = jnp.where(diff >= 0, jnp.exp(jnp.maximum(diff, 0.0) * log_gamma[:, None, None]), 0.0)
    zeta = jnp.exp((RET_CHUNK - 1 - idx) * log_gamma[:, None])
    xi = jnp.exp((idx + 1.0) * log_gamma[:, None])
    cd = jnp.exp(RET_CHUNK * log_gamma)
    xi_b = jnp.broadcast_to(xi[:, :, None], (RET_HEADS, RET_CHUNK, RET_DV))
    zeta_b = jnp.broadcast_to(zeta[:, :, None], (RET_HEADS, RET_CHUNK, RET_QK))
    cd_b = jnp.broadcast_to(cd[:, None, None], (RET_HEADS, 1, RET_DV))
    lane = jnp.arange(RET_QK)
    head_of_lane = (lane % (RET_QK // 2)) // (RET_DK // 2)
    mask = (head_of_lane[None, :] == jnp.arange(RET_HEADS)[:, None]).astype(F32)[:, None, :]
    return dec, xi_b, zeta_b, cd_b, mask


def _post_kernel(h_ref, o_ref, y_ref, pre_ref, wg_ref, gn_ref, wbm_ref, wbr_ref, wout_ref, post_ref,
                 out_ref, *, parts):
    rows = h_ref.shape[0] // parts
    sl = [slice(p * rows, (p + 1) * rows) for p in range(parts)]
    hn = [_rms(h_ref[s, :], pre_ref[...]).astype(BF16) for s in sl]
    rg = [_dot(x, wg_ref[:, :RET_V]) for x in hn]
    o_mla = [_dot(o_ref[s, :], wbm_ref[...]) for s in sl]
    gates = [_dot(x, wg_ref[:, RET_V:]) for x in hn]
    o_ret = []
    for p, s in enumerate(sl):
        a = (rg[p] * jax.nn.sigmoid(rg[p])) * (y_ref[s, :] * gn_ref[...])
        o_ret.append(_dot(a.astype(BF16), wbr_ref[...]))
    m = []
    for p in range(parts):
        merged = (jax.nn.sigmoid(gates[p][:, :D_MODEL]) * o_mla[p]
                  + jax.nn.sigmoid(gates[p][:, D_MODEL:]) * o_ret[p])
        m.append(_dot(merged.astype(BF16), wout_ref[...]))
    for p, s in enumerate(sl):
        out_ref[s, :] = h_ref[s, :] + _rms(m[p], post_ref[...])


def _mixer_post(h, o, y, pre_w, w_gates, gn_w, w_bm, w_br, w_out, post_w, tm):
    t = h.shape[0]

    def row(width):
        return pl.BlockSpec((tm, width), lambda i: (i, 0))

    return pl.pallas_call(
        functools.partial(_post_kernel, parts=POST_PARTS),
        grid=(t // tm,),
        in_specs=[row(D_MODEL), row(MLA_HEADS * MLA_V), row(RET_V), _const_spec((1, D_MODEL)),
                  _const_spec(w_gates.shape), _const_spec((1, RET_V)), _const_spec(w_bm.shape),
                  _const_spec(w_br.shape), _const_spec(w_out.shape), _const_spec((1, D_MODEL))],
        out_specs=row(D_MODEL),
        out_shape=jax.ShapeDtypeStruct((t, D_MODEL), F32),
        compiler_params=pltpu.CompilerParams(dimension_semantics=("parallel",),
                                             vmem_limit_bytes=VMEM_LIMIT),
        name="mixer_post",
    )(h, o, y, pre_w, w_gates, gn_w, w_bm, w_br, w_out, post_w)


def _rope_rows():
    lane = jnp.arange(LANES)
    half_m, half_r = MLA_ROPE // 2, RET_DK // 2
    f_mla = ROPE_BASE ** (-jnp.arange(half_m, dtype=F32) / half_m)
    f_ret = ROPE_BASE ** (-jnp.arange(half_r, dtype=F32) / half_r)
    token = jnp.concatenate([f_ret, f_mla, f_mla, jnp.zeros((LANES // 2,), F32)])
    in_tok = lane % (LANES // 2)
    m_ret = (in_tok < half_r).astype(F32)
    m_mla = (in_tok >= half_r).astype(F32)
    first = (lane < LANES // 2).astype(F32)
    sign = jnp.where(in_tok < half_r, 0.0, jnp.where(in_tok < half_r + half_m, -1.0, 1.0))
    rows = [token, (lane < MLA_NOPE).astype(F32), m_mla * first, m_ret * first,
            jnp.roll(token, LANES // 2), m_mla * (1 - first), m_ret * (1 - first), sign]
    return jnp.stack(rows)


def _split_halves(w, heads, dim):
    h = dim // 2
    return ([w[:, hd * dim:hd * dim + h] for hd in range(heads)]
            + [w[:, hd * dim + h:(hd + 1) * dim] for hd in range(heads)])


def _rope_cols(w):
    h = MLA_ROPE // 2
    return jnp.concatenate([w, w[..., h:], w[..., :h]], axis=-1)


def _layout_w_in(w_in):
    sizes = (MLA_Q_RANK, MLA_KV_RANK, MLA_ROPE, RET_QK, RET_QK, RET_V, RET_V, D_MODEL, D_MODEL)
    parts, off = [], 0
    for s in sizes:
        parts.append(w_in[:, off:off + s])
        off += s
    w_cq, w_ckv, w_kr, w_rq, w_rk, w_rv = parts[:6]
    rows = w_in.shape[0]
    w_pre = jnp.concatenate([w_cq, w_ckv, jnp.zeros((rows, MLA_NOPE), w_in.dtype), _rope_cols(w_kr)]
                            + _split_halves(w_rq, RET_HEADS, RET_DK)
                            + _split_halves(w_rk, RET_HEADS, RET_DK) + [w_rv], axis=1)
    w_gates = w_in[:, w_in.shape[1] - (RET_V + 2 * D_MODEL):]
    return w_pre.astype(BF16), w_gates.astype(BF16)


def _layout_w_uq(w_uq):
    w = w_uq.reshape(MLA_Q_RANK, MLA_HEADS, MLA_NOPE + MLA_ROPE)
    w = jnp.concatenate([w[..., :MLA_NOPE], _rope_cols(w[..., MLA_NOPE:])], axis=-1)
    return w.reshape(MLA_Q_RANK, MLA_HEADS * HEAD_PAD).astype(BF16)


def _layout_w_ukv(w_ukv):
    w = w_ukv.reshape(MLA_KV_RANK, MLA_HEADS, MLA_NOPE + MLA_V)
    w_k = jnp.pad(w[:, :, :MLA_NOPE], ((0, 0), (0, 0), (0, HEAD_PAD - MLA_NOPE)))
    w_v = jnp.pad(w[:, :, MLA_NOPE:], ((0, 0), (0, 0), (0, HEAD_PAD - MLA_V)))
    w_k = w_k.reshape(MLA_KV_RANK, -1).astype(BF16)
    w_vt = w_v.reshape(MLA_KV_RANK, -1).T.astype(BF16)
    return w_k, w_vt


def _pack_positions(positions, tk):
    b, s = positions.shape
    p = positions.reshape(b * s // tk, 2, tk // 2).astype(F32)
    first = p[:, 0, :].reshape(b * s // 2, 1)
    second = p[:, 1, :].reshape(b * s // 2, 1)
    lane = lax.broadcasted_iota(jnp.int32, (b * s // 2, LANES), 1)
    return jnp.where(lane < LANES // 2, first, second)


def _row(w):
    return w.reshape(1, -1)


def kernel(x, positions, ffn1_pre_w, ffn1_w1, ffn1_w2, ffn1_post_w, mix_pre_w, w_in, mla_q_norm_w,
           mla_w_uq, mla_kv_norm_w, mla_w_ukv, ret_gn_w, w_branch_mla, w_branch_ret, w_out, mix_post_w,
           ffn2_pre_w, ffn2_w1, ffn2_w2, ffn2_post_w):
    batch, seq, _ = x.shape
    depth = ffn1_w1.shape[0]
    t = batch * seq
    tm = min(RET_TILE, seq)
    tm_ffn = min(DENSE_TILE, seq)
    tq = min(ATTN_Q_TILE, seq)
    kb = min(ATTN_KEY_BLOCK, seq)
    h = x.reshape(t, D_MODEL)
    pos = _pack_positions(positions, kb)
    rows = _rope_rows()
    ret_consts = _retention_consts()
    for l in range(depth):
        h = _ffn(h, _row(ffn1_pre_w[l]), ffn1_w1[l].astype(BF16), ffn1_w2[l].astype(BF16),
                 _row(ffn1_post_w[l]), tm_ffn)
        w_pre, w_gates = _layout_w_in(w_in[l])
        w_uk, w_uvt = _layout_w_ukv(mla_w_ukv[l])
        q, k, vt, rq, rk, rv = _mixer_pre(h, pos, rows, _row(mix_pre_w[l]), w_pre, _row(mla_q_norm_w[l]),
                                          _layout_w_uq(mla_w_uq[l]), _row(mla_kv_norm_w[l]), w_uk, w_uvt,
                                          batch, seq, tm_ffn, kb)
        o = _attention(q, k, vt, batch, seq, tq, kb)
        y = _retention(rq, rk, rv, ret_consts, batch, seq, tm)
        h = _mixer_post(h, o, y, _row(mix_pre_w[l]), w_gates, _row(ret_gn_w[l]),
                        w_branch_mla[l].astype(BF16), w_branch_ret[l].astype(BF16), w_out[l].astype(BF16),
                        _row(mix_post_w[l]), tm_ffn)
        h = _ffn(h, _row(ffn2_pre_w[l]), ffn2_w1[l].astype(BF16), ffn2_w2[l].astype(BF16),
                 _row(ffn2_post_w[l]), tm_ffn)
    return h.reshape(batch, seq, D_MODEL)
```

```python
import functools
import math

import jax
import jax.numpy as jnp
from jax import lax
from jax.experimental import pallas as pl
from jax.experimental.pallas import tpu as pltpu

D_MODEL = 1024
MLA_HEADS = 8
MLA_NOPE = 64
MLA_ROPE = 32
MLA_V = 64
MLA_Q_RANK = 384
MLA_KV_RANK = 256
RET_HEADS = 4
RET_DK = 64
RET_DV = 128
RET_CHUNK = 128
FF_DIM = 2816
ROPE_BASE = 10000.0
NORM_EPS = 1e-6
GN_EPS = 1e-6

LANES = 128
HEAD_PAD = 128
RET_QK = RET_HEADS * RET_DK
RET_V = RET_HEADS * RET_DV
VMEM_LIMIT = 56 * 1024 * 1024
DENSE_TILE = 1024
RET_TILE = 1024
ATTN_Q_TILE = 1024
ATTN_KEY_BLOCK = 256
FFN_PARTS = 4
FFN_W1_CHUNK = 64
FFN_W2_CHUNK = 256
POST_PARTS = 4
ATTN_HEADS_PER_STEP = 4

ATTN_LOG2_SCALE = math.log2(math.e) / math.sqrt(MLA_NOPE + MLA_ROPE)

BF16 = jnp.bfloat16
F32 = jnp.float32

PRE_CQ = 0
PRE_CKV = PRE_CQ + MLA_Q_RANK
PRE_KR = PRE_CKV + MLA_KV_RANK
PRE_RQ = PRE_KR + HEAD_PAD
PRE_RK = PRE_RQ + RET_QK
PRE_RV = PRE_RK + RET_QK
PRE_WIDTH = PRE_RV + RET_V


def _rms(x, w):
    return x * lax.rsqrt(jnp.mean(x * x, axis=-1, keepdims=True) + NORM_EPS) * w


def _dot(a, b):
    return jnp.dot(a, b, preferred_element_type=F32)


def _dot_nt(a, b):
    return lax.dot_general(a, b, (((1,), (1,)), ((), ())), preferred_element_type=F32)


def _dot_tn(a, b):
    return lax.dot_general(a, b, (((0,), (0,)), ((), ())), preferred_element_type=F32)


def _const_spec(shape):
    nd = len(shape)
    return pl.BlockSpec(shape, lambda *_: (0,) * nd, pipeline_mode=pl.Buffered(1))


def _load_as_bf16(w_hbm, w_ref, stage, sem):
    chunk = stage.shape[1]
    n = w_hbm.shape[0] // chunk

    def copy(i):
        return pltpu.make_async_copy(w_hbm.at[pl.ds(i * chunk, chunk), :], stage.at[i % 2], sem.at[i % 2])

    copy(0).start()
    for i in range(n):
        if i + 1 < n:
            copy(i + 1).start()
        copy(i).wait()
        w_ref[i * chunk:(i + 1) * chunk, :] = stage[i % 2].astype(BF16)


def _ffn_kernel(h_ref, pre_ref, w1_hbm, w2_hbm, post_ref, o_ref, w1_ref, w2_ref, stage1, stage2, sem1, sem2,
                *, parts):
    @pl.when(pl.program_id(0) == 0)
    def _():
        _load_as_bf16(w1_hbm, w1_ref, stage1, sem1)
        _load_as_bf16(w2_hbm, w2_ref, stage2, sem2)

    rows = h_ref.shape[0] // parts
    sl = [slice(p * rows, (p + 1) * rows) for p in range(parts)]
    xn = [_rms(h_ref[s, :], pre_ref[...]).astype(BF16) for s in sl]
    gu = [_dot(x, w1_ref[...]) for x in xn]
    f = []
    for p in range(parts):
        g, u = gu[p][:, :FF_DIM], gu[p][:, FF_DIM:]
        f.append(_dot((g * jax.nn.sigmoid(g) * u).astype(BF16), w2_ref[...]))
    for p, s in enumerate(sl):
        o_ref[s, :] = h_ref[s, :] + 0.5 * _rms(f[p], post_ref[...])


def _ffn(h, pre_w, w1, w2, post_w, tm):
    t = h.shape[0]
    row = pl.BlockSpec((tm, D_MODEL), lambda i: (i, 0))
    return pl.pallas_call(
        functools.partial(_ffn_kernel, parts=FFN_PARTS),
        grid=(t // tm,),
        in_specs=[row, _const_spec((1, D_MODEL)), pl.BlockSpec(memory_space=pl.ANY),
                  pl.BlockSpec(memory_space=pl.ANY), _const_spec((1, D_MODEL))],
        out_specs=row,
        out_shape=jax.ShapeDtypeStruct((t, D_MODEL), F32),
        scratch_shapes=[pltpu.VMEM((D_MODEL, 2 * FF_DIM), BF16), pltpu.VMEM((FF_DIM, D_MODEL), BF16),
                        pltpu.VMEM((2, FFN_W1_CHUNK, 2 * FF_DIM), F32),
                        pltpu.VMEM((2, FFN_W2_CHUNK, D_MODEL), F32),
                        pltpu.SemaphoreType.DMA((2,)), pltpu.SemaphoreType.DMA((2,))],
        compiler_params=pltpu.CompilerParams(dimension_semantics=("arbitrary",),
                                             vmem_limit_bytes=VMEM_LIMIT),
        name="ffn",
    )(h, pre_w, w1, w2, post_w)


def _rope_lanes(x, c, s):
    return x * c + pltpu.roll(x, LANES - MLA_ROPE, 1) * s


def _tile4(x):
    y = x + pltpu.roll(x, 32, 1)
    return y + pltpu.roll(y, 64, 1)


def _pre_kernel(h_ref, pos_ref, rows_ref, pre_ref, win_ref, qn_ref, wuq_ref, kvn_ref, wuk_ref, wuvt_ref,
                q_ref, k_ref, vt_ref, rq_ref, rk_ref, rv_ref, *, tk):
    parts = h_ref.shape[0] // tk
    sl = [slice(p * tk, (p + 1) * tk) for p in range(parts)]
    rows = rows_ref[...]
    half = RET_QK // 2
    k_scale = RET_DK ** -0.5

    hn = [_rms(h_ref[s, :], pre_ref[...]).astype(BF16) for s in sl]
    lat = [_dot(x, win_ref[:, :PRE_RQ]) for x in hn]

    up = []
    for p in range(parts):
        cq = _rms(lat[p][:, PRE_CQ:PRE_CKV], qn_ref[...]).astype(BF16)
        ckv = _rms(lat[p][:, PRE_CKV:PRE_KR], kvn_ref[...]).astype(BF16)
        up.append((_dot(cq, wuq_ref[...]), _dot(ckv, wuk_ref[...]), _dot_nt(wuvt_ref[...], ckv)))

    ret = [_dot(x, win_ref[:, PRE_RQ:]) for x in hn]

    for p, s in enumerate(sl):
        q_raw, k_raw, vt = up[p]
        ang = pos_ref[p * tk // 2:(p + 1) * tk // 2, :] * (rows[0:1, :] + rows[4:5, :])
        cos, sin = jnp.cos(ang), jnp.sin(ang)
        tabs = []
        for m_mla, m_ret, shift in ((rows[2:3, :], rows[3:4, :], 32), (rows[5:6, :], rows[6:7, :], 96)):
            tabs.append((rows[1:2, :] + pltpu.roll(cos * m_mla, shift, 1),
                         pltpu.roll(sin * (m_mla * rows[7:8, :]), shift, 1),
                         _tile4(cos * m_ret), _tile4(sin * m_ret)))
        c_q, s_q, c_r, s_r = (jnp.concatenate([a, b], axis=0) for a, b in zip(*tabs))

        kr = _rope_lanes(lat[p][:, PRE_KR:PRE_RQ], c_q, s_q)
        for hd in range(MLA_HEADS):
            cols = slice(hd * HEAD_PAD, (hd + 1) * HEAD_PAD)
            q_ref[s, cols] = (_rope_lanes(q_raw[:, cols], c_q, s_q) * ATTN_LOG2_SCALE).astype(BF16)
            k_ref[s, cols] = (k_raw[:, cols] + kr).astype(BF16)
        row = lax.broadcasted_iota(jnp.int32, vt.shape, 0)
        vt = jnp.where(row % HEAD_PAD == MLA_V, 1.0, vt).astype(BF16)
        for hd in range(MLA_HEADS):
            vt_ref[0, hd, p] = vt[hd * HEAD_PAD:(hd + 1) * HEAD_PAD, :]

        rq1, rq2 = ret[p][:, :half], ret[p][:, half:RET_QK]
        rq_ref[s, :half] = (rq1 * c_r - rq2 * s_r).astype(BF16)
        rq_ref[s, half:] = (rq2 * c_r + rq1 * s_r).astype(BF16)
        rk1, rk2 = ret[p][:, RET_QK:RET_QK + half], ret[p][:, RET_QK + half:2 * RET_QK]
        rk_ref[s, :half] = (rk1 * c_r - rk2 * s_r) * k_scale
        rk_ref[s, half:] = (rk2 * c_r + rk1 * s_r) * k_scale
        rv_ref[s, :] = ret[p][:, 2 * RET_QK:].astype(BF16)


def _mixer_pre(h, pos, rows, pre_w, w_in_pre, qn_w, w_uq, kvn_w, w_uk, w_uvt, batch, seq, tm, tk):
    t = h.shape[0]
    ns = seq // tm

    def row(width):
        return pl.BlockSpec((tm, width), lambda i: (i, 0))

    qk_w = MLA_HEADS * HEAD_PAD
    vt_shape = (batch, MLA_HEADS, seq // tk, HEAD_PAD, tk)
    vt_spec = pl.BlockSpec((1, MLA_HEADS, tm // tk, HEAD_PAD, tk), lambda i: (i // ns, 0, i % ns, 0, 0))
    return pl.pallas_call(
        functools.partial(_pre_kernel, tk=tk),
        grid=(t // tm,),
        in_specs=[row(D_MODEL), pl.BlockSpec((tm // 2, LANES), lambda i: (i, 0)),
                  _const_spec((8, LANES)), _const_spec((1, D_MODEL)),
                  _const_spec((D_MODEL, PRE_WIDTH)), _const_spec((1, MLA_Q_RANK)),
                  _const_spec((MLA_Q_RANK, qk_w)), _const_spec((1, MLA_KV_RANK)),
                  _const_spec((MLA_KV_RANK, qk_w)), _const_spec((qk_w, MLA_KV_RANK))],
        out_specs=[row(qk_w), row(qk_w), vt_spec, row(RET_QK), row(RET_QK), row(RET_V)],
        out_shape=[jax.ShapeDtypeStruct((t, qk_w), BF16), jax.ShapeDtypeStruct((t, qk_w), BF16),
                   jax.ShapeDtypeStruct(vt_shape, BF16), jax.ShapeDtypeStruct((t, RET_QK), BF16),
                   jax.ShapeDtypeStruct((t, RET_QK), F32), jax.ShapeDtypeStruct((t, RET_V), BF16)],
        compiler_params=pltpu.CompilerParams(dimension_semantics=("parallel",),
                                             vmem_limit_bytes=VMEM_LIMIT),
        name="mixer_pre",
    )(h, pos, rows, pre_w, w_in_pre, qn_w, w_uq, kvn_w, w_uk, w_uvt)


def _attn_kernel(q_ref, k_ref, vt_ref, o_ref, sa_ref, sb_ref, m_ref, acc_ref, *, tq, kb):
    qi = pl.program_id(2)
    heads = q_ref.shape[1] // HEAD_PAD
    nb = tq // (2 * kb)

    def qk(blk, n, s_ref, row0=0, q_lo=0):
        start = pl.multiple_of(blk * kb, kb)
        tile_max = []
        for hd in range(heads):
            cols = slice(hd * HEAD_PAD, (hd + 1) * HEAD_PAD)
            s = _dot_nt(k_ref[pl.ds(start, n * kb), cols], q_ref[q_lo:, cols])
            s_ref[hd, row0:row0 + n * kb, q_lo:] = s
            tile_max.append(jnp.max(s, axis=0, keepdims=True))
        return tuple(tile_max)

    def softmax_pv(blk, n, s_ref, tile_max, row0=0, q_lo=0, masked=False):
        for hd in range(heads):
            s = s_ref[hd, row0:row0 + n * kb, q_lo:]
            if masked:
                keep = (lax.broadcasted_iota(jnp.int32, s.shape, 0)
                        <= lax.broadcasted_iota(jnp.int32, s.shape, 1))
                s = jnp.where(keep, s, -jnp.inf)
                s_max = jnp.max(s, axis=0, keepdims=True)
            else:
                s_max = tile_max[hd]
            m = m_ref[hd, :, q_lo:]
            m_new = jnp.maximum(m, s_max)
            m_ref[hd, :, q_lo:] = m_new
            p = jnp.exp2(s - m_new).astype(BF16)
            vt = jnp.concatenate([vt_ref[hd, blk + i] for i in range(n)], axis=1)
            acc_ref[hd, :, q_lo:] = (jnp.exp2(m - m_new) * acc_ref[hd, :, q_lo:]
                                     + _dot(vt, p))

    def step(jj, max_a):
        blk = 2 * nb * jj
        max_b = qk(blk + nb, nb, sb_ref)
        softmax_pv(blk, nb, sa_ref, max_a)
        max_a = qk(blk + 2 * nb, nb, sa_ref)
        softmax_pv(blk + nb, nb, sb_ref, max_b)
        return max_a

    m_ref[...] = jnp.full(m_ref.shape, -jnp.inf, F32)
    acc_ref[...] = jnp.zeros(acc_ref.shape, F32)
    lax.fori_loop(0, qi, step, qk(0, nb, sa_ref))
    blk = 2 * nb * qi
    qk(blk + nb, nb, sb_ref, q_lo=nb * kb)
    softmax_pv(blk, nb, sa_ref, None, masked=True)
    softmax_pv(blk + nb, nb, sb_ref, None, q_lo=nb * kb, masked=True)
    o_t = jnp.concatenate([acc_ref[hd, :MLA_V, :] / acc_ref[hd, MLA_V:MLA_V + 1, :] for hd in range(heads)],
                          axis=0)
    o_ref[...] = o_t.T.astype(o_ref.dtype)


def _attention(q, k, vt, batch, seq, tq, kb):
    t = q.shape[0]
    nq = seq // tq
    hg = ATTN_HEADS_PER_STEP
    return pl.pallas_call(
        functools.partial(_attn_kernel, tq=tq, kb=kb),
        grid=(batch, MLA_HEADS // hg, nq),
        in_specs=[pl.BlockSpec((tq, hg * HEAD_PAD), lambda b, p, i: (b * nq + i, p)),
                  pl.BlockSpec((seq, hg * HEAD_PAD), lambda b, p, i: (b, p)),
                  pl.BlockSpec((None, hg, seq // kb, HEAD_PAD, kb), lambda b, p, i: (b, p, 0, 0, 0))],
        out_specs=pl.BlockSpec((tq, hg * MLA_V), lambda b, p, i: (b * nq + i, p)),
        out_shape=jax.ShapeDtypeStruct((t, MLA_HEADS * MLA_V), BF16),
        compiler_params=pltpu.CompilerParams(
            dimension_semantics=("parallel", "parallel", "arbitrary"),
            vmem_limit_bytes=VMEM_LIMIT),
        scratch_shapes=[pltpu.VMEM((hg, tq // 2, tq), F32), pltpu.VMEM((hg, tq // 2, tq), F32),
                        pltpu.VMEM((hg, 1, tq), F32), pltpu.VMEM((hg, HEAD_PAD, tq), F32)],
        name="attention",
    )(q, k, vt)


def _ret_kernel(rq_ref, rk_ref, rv_ref, dec_ref, xi_ref, zeta_ref, cd_ref, mask_ref, y_ref, state_ref,
                *, chunks):
    @pl.when(pl.program_id(1) == 0)
    def _():
        state_ref[...] = jnp.zeros(state_ref.shape, F32)

    blocks = [(c, hd) for c in range(chunks) for hd in range(RET_HEADS)]
    rows = [slice(c * RET_CHUNK, (c + 1) * RET_CHUNK) for c in range(chunks)]
    cols = [slice(hd * RET_DV, (hd + 1) * RET_DV) for hd in range(RET_HEADS)]
    q = [rq_ref[r, :] for r in rows]
    scores, kv = {}, {}
    for c, hd in blocks:
        km = rk_ref[rows[c], :] * mask_ref[hd]
        v = rv_ref[rows[c], cols[hd]]
        scores[c, hd] = _dot_nt(q[c], km.astype(BF16))
        kv[c, hd] = _dot_tn((km * zeta_ref[hd]).astype(BF16), v)
    prev = {}
    for hd in range(RET_HEADS):
        state = state_ref[hd]
        for c in range(chunks):
            prev[c, hd] = state
            state = state * cd_ref[hd] + kv[c, hd]
        state_ref[hd] = state
    for c, hd in blocks:
        inner = (scores[c, hd] * dec_ref[hd]).astype(BF16)
        y = (_dot(inner, rv_ref[rows[c], cols[hd]])
             + _dot(q[c], prev[c, hd].astype(BF16)) * xi_ref[hd])
        mu = jnp.mean(y, axis=-1, keepdims=True)
        yc = y - mu
        var = jnp.mean(yc * yc, axis=-1, keepdims=True)
        y_ref[rows[c], cols[hd]] = yc * lax.rsqrt(var + GN_EPS)


def _retention(rq, rk, rv, consts, batch, seq, tm):
    t = rq.shape[0]
    ns = seq // tm
    dec, xi, zeta, cd, mask = consts

    def row(width):
        return pl.BlockSpec((tm, width), lambda b, i: (b * ns + i, 0))

    return pl.pallas_call(
        functools.partial(_ret_kernel, chunks=tm // RET_CHUNK),
        grid=(batch, ns),
        in_specs=[row(RET_QK), row(RET_QK), row(RET_V), _const_spec(dec.shape), _const_spec(xi.shape),
                  _const_spec(zeta.shape), _const_spec(cd.shape), _const_spec(mask.shape)],
        out_specs=row(RET_V),
        out_shape=jax.ShapeDtypeStruct((t, RET_V), F32),
        scratch_shapes=[pltpu.VMEM((RET_HEADS, RET_QK, RET_DV), F32)],
        compiler_params=pltpu.CompilerParams(dimension_semantics=("parallel", "arbitrary"),
                                             vmem_limit_bytes=VMEM_LIMIT),
        name="retention",
    )(rq, rk, rv, dec, xi, zeta, cd, mask)


def _retention_consts():
    hh = jnp.arange(RET_HEADS, dtype=F32)
    log_gamma = jnp.log(1.0 - 2.0 ** (-5.0 - hh))
    idx = jnp.arange(RET_CHUNK, dtype=F32)
    diff = idx[:, None] - idx[None, :]
    dec = jnp.where(diff >= 0, jnp.exp(jnp.maximum(diff, 0.0) * log_gamma[:, None, None]), 0.0)
    zeta = jnp.exp((RET_CHUNK - 1 - idx) * log_gamma[:, None])
    xi = jnp.exp((idx + 1.0) * log_gamma[:, None])
    cd = jnp.exp(RET_CHUNK * log_gamma)
    xi_b = jnp.broadcast_to(xi[:, :, None], (RET_HEADS, RET_CHUNK, RET_DV))
    zeta_b = jnp.broadcast_to(zeta[:, :, None], (RET_HEADS, RET_CHUNK, RET_QK))
    cd_b = jnp.broadcast_to(cd[:, None, None], (RET_HEADS, 1, RET_DV))
    lane = jnp.arange(RET_QK)
    head_of_lane = (lane % (RET_QK // 2)) // (RET_DK // 2)
    mask = (head_of_lane[None, :] == jnp.arange(RET_HEADS)[:, None]).astype(F32)[:, None, :]
    return dec, xi_b, zeta_b, cd_b, mask


def _post_kernel(h_ref, o_ref, y_ref, pre_ref, wg_ref, gn_ref, wbm_ref, wbr_ref, wout_ref, post_ref,
                 out_ref, *, parts):
    rows = h_ref.shape[0] // parts
    sl = [slice(p * rows, (p + 1) * rows) for p in range(parts)]
    hn = [_rms(h_ref[s, :], pre_ref[...]).astype(BF16) for s in sl]
    rg = [_dot(x, wg_ref[:, :RET_V]) for x in hn]
    o_mla = [_dot(o_ref[s, :], wbm_ref[...]) for s in sl]
    gates = [_dot(x, wg_ref[:, RET_V:]) for x in hn]
    o_ret = []
    for p, s in enumerate(sl):
        a = (rg[p] * jax.nn.sigmoid(rg[p])) * (y_ref[s, :] * gn_ref[...])
        o_ret.append(_dot(a.astype(BF16), wbr_ref[...]))
    m = []
    for p in range(parts):
        merged = (jax.nn.sigmoid(gates[p][:, :D_MODEL]) * o_mla[p]
                  + jax.nn.sigmoid(gates[p][:, D_MODEL:]) * o_ret[p])
        m.append(_dot(merged.astype(BF16), wout_ref[...]))
    for p, s in enumerate(sl):
        out_ref[s, :] = h_ref[s, :] + _rms(m[p], post_ref[...])


def _mixer_post(h, o, y, pre_w, w_gates, gn_w, w_bm, w_br, w_out, post_w, tm):
    t = h.shape[0]

    def row(width):
        return pl.BlockSpec((tm, width), lambda i: (i, 0))

    return pl.pallas_call(
        functools.partial(_post_kernel, parts=POST_PARTS),
        grid=(t // tm,),
        in_specs=[row(D_MODEL), row(MLA_HEADS * MLA_V), row(RET_V), _const_spec((1, D_MODEL)),
                  _const_spec(w_gates.shape), _const_spec((1, RET_V)), _const_spec(w_bm.shape),
                  _const_spec(w_br.shape), _const_spec(w_out.shape), _const_spec((1, D_MODEL))],
        out_specs=row(D_MODEL),
        out_shape=jax.ShapeDtypeStruct((t, D_MODEL), F32),
        compiler_params=pltpu.CompilerParams(dimension_semantics=("parallel",),
                                             vmem_limit_bytes=VMEM_LIMIT),
        name="mixer_post",
    )(h, o, y, pre_w, w_gates, gn_w, w_bm, w_br, w_out, post_w)


def _rope_rows():
    lane = jnp.arange(LANES)
    half_m, half_r = MLA_ROPE // 2, RET_DK // 2
    f_mla = ROPE_BASE ** (-jnp.arange(half_m, dtype=F32) / half_m)
    f_ret = ROPE_BASE ** (-jnp.arange(half_r, dtype=F32) / half_r)
    token = jnp.concatenate([f_ret, f_mla, f_mla, jnp.zeros((LANES // 2,), F32)])
    in_tok = lane % (LANES // 2)
    m_ret = (in_tok < half_r).astype(F32)
    m_mla = (in_tok >= half_r).astype(F32)
    first = (lane < LANES // 2).astype(F32)
    sign = jnp.where(in_tok < half_r, 0.0, jnp.where(in_tok < half_r + half_m, -1.0, 1.0))
    rows = [token, (lane < MLA_NOPE).astype(F32), m_mla * first, m_ret * first,
            jnp.roll(token, LANES // 2), m_mla * (1 - first), m_ret * (1 - first), sign]
    return jnp.stack(rows)


def _split_halves(w, heads, dim):
    h = dim // 2
    return ([w[:, hd * dim:hd * dim + h] for hd in range(heads)]
            + [w[:, hd * dim + h:(hd + 1) * dim] for hd in range(heads)])


def _rope_cols(w):
    h = MLA_ROPE // 2
    return jnp.concatenate([w, w[..., h:], w[..., :h]], axis=-1)


def _layout_w_in(w_in):
    sizes = (MLA_Q_RANK, MLA_KV_RANK, MLA_ROPE, RET_QK, RET_QK, RET_V, RET_V, D_MODEL, D_MODEL)
    parts, off = [], 0
    for s in sizes:
        parts.append(w_in[:, off:off + s])
        off += s
    w_cq, w_ckv, w_kr, w_rq, w_rk, w_rv = parts[:6]
    rows = w_in.shape[0]
    w_pre = jnp.concatenate([w_cq, w_ckv, jnp.zeros((rows, MLA_NOPE), w_in.dtype), _rope_cols(w_kr)]
                            + _split_halves(w_rq, RET_HEADS, RET_DK)
                            + _split_halves(w_rk, RET_HEADS, RET_DK) + [w_rv], axis=1)
    w_gates = w_in[:, w_in.shape[1] - (RET_V + 2 * D_MODEL):]
    return w_pre.astype(BF16), w_gates.astype(BF16)


def _layout_w_uq(w_uq):
    w = w_uq.reshape(MLA_Q_RANK, MLA_HEADS, MLA_NOPE + MLA_ROPE)
    w = jnp.concatenate([w[..., :MLA_NOPE], _rope_cols(w[..., MLA_NOPE:])], axis=-1)
    return w.reshape(MLA_Q_RANK, MLA_HEADS * HEAD_PAD).astype(BF16)


def _layout_w_ukv(w_ukv):
    w = w_ukv.reshape(MLA_KV_RANK, MLA_HEADS, MLA_NOPE + MLA_V)
    w_k = jnp.pad(w[:, :, :MLA_NOPE], ((0, 0), (0, 0), (0, HEAD_PAD - MLA_NOPE)))
    w_v = jnp.pad(w[:, :, MLA_NOPE:], ((0, 0), (0, 0), (0, HEAD_PAD - MLA_V)))
    w_k = w_k.reshape(MLA_KV_RANK, -1).astype(BF16)
    w_vt = w_v.reshape(MLA_KV_RANK, -1).T.astype(BF16)
    return w_k, w_vt


def _pack_positions(positions, tk):
    b, s = positions.shape
    p = positions.reshape(b * s // tk, 2, tk // 2).astype(F32)
    first = p[:, 0, :].reshape(b * s // 2, 1)
    second = p[:, 1, :].reshape(b * s // 2, 1)
    lane = lax.broadcasted_iota(jnp.int32, (b * s // 2, LANES), 1)
    return jnp.where(lane < LANES // 2, first, second)


def _row(w):
    return w.reshape(1, -1)


def kernel(x, positions, ffn1_pre_w, ffn1_w1, ffn1_w2, ffn1_post_w, mix_pre_w, w_in, mla_q_norm_w,
           mla_w_uq, mla_kv_norm_w, mla_w_ukv, ret_gn_w, w_branch_mla, w_branch_ret, w_out, mix_post_w,
           ffn2_pre_w, ffn2_w1, ffn2_w2, ffn2_post_w):
    batch, seq, _ = x.shape
    depth = ffn1_w1.shape[0]
    t = batch * seq
    tm = min(RET_TILE, seq)
    tm_ffn = min(DENSE_TILE, seq)
    tq = min(ATTN_Q_TILE, seq)
    kb = min(ATTN_KEY_BLOCK, seq)
    h = x.reshape(t, D_MODEL)
    pos = _pack_positions(positions, kb)
    rows = _rope_rows()
    ret_consts = _retention_consts()
    for l in range(depth):
        h = _ffn(h, _row(ffn1_pre_w[l]), ffn1_w1[l], ffn1_w2[l],
                 _row(ffn1_post_w[l]), tm_ffn)
        w_pre, w_gates = _layout_w_in(w_in[l])
        w_uk, w_uvt = _layout_w_ukv(mla_w_ukv[l])
        q, k, vt, rq, rk, rv = _mixer_pre(h, pos, rows, _row(mix_pre_w[l]), w_pre, _row(mla_q_norm_w[l]),
                                          _layout_w_uq(mla_w_uq[l]), _row(mla_kv_norm_w[l]), w_uk, w_uvt,
                                          batch, seq, tm_ffn, kb)
        o = _attention(q, k, vt, batch, seq, tq, kb)
        y = _retention(rq, rk, rv, ret_consts, batch, seq, tm)
        h = _mixer_post(h, o, y, _row(mix_pre_w[l]), w_gates, _row(ret_gn_w[l]),
                        w_branch_mla[l].astype(BF16), w_branch_ret[l].astype(BF16), w_out[l].astype(BF16),
                        _row(mix_post_w[l]), tm_ffn)
        h = _ffn(h, _row(ffn2_pre_w[l]), ffn2_w1[l], ffn2_w2[l],
                 _row(ffn2_post_w[l]), tm_ffn)
    return h.reshape(batch, seq, D_MODEL)
```

```python
import functools
import math

import jax
import jax.numpy as jnp
from jax import lax
from jax.experimental import pallas as pl
from jax.experimental.pallas import tpu as pltpu

D_MODEL = 1024
MLA_HEADS = 8
MLA_NOPE = 64
MLA_ROPE = 32
MLA_V = 64
MLA_Q_RANK = 384
MLA_KV_RANK = 256
RET_HEADS = 4
RET_DK = 64
RET_DV = 128
RET_CHUNK = 128
FF_DIM = 2816
ROPE_BASE = 10000.0
NORM_EPS = 1e-6
GN_EPS = 1e-6

LANES = 128
HEAD_PAD = 128
RET_QK = RET_HEADS * RET_DK
RET_V = RET_HEADS * RET_DV
VMEM_LIMIT = 56 * 1024 * 1024
DENSE_TILE = 1024
RET_TILE = 1024
ATTN_Q_TILE = 1024
ATTN_KEY_BLOCK = 256
FFN_PARTS = 4
FFN_W1_CHUNK = 32
FFN_W2_CHUNK = 128
FFN_W_SLOTS = 4
POST_PARTS = 4
ATTN_HEADS_PER_STEP = 4

ATTN_LOG2_SCALE = math.log2(math.e) / math.sqrt(MLA_NOPE + MLA_ROPE)

BF16 = jnp.bfloat16
F32 = jnp.float32

PRE_CQ = 0
PRE_CKV = PRE_CQ + MLA_Q_RANK
PRE_KR = PRE_CKV + MLA_KV_RANK
PRE_RQ = PRE_KR + HEAD_PAD
PRE_RK = PRE_RQ + RET_QK
PRE_RV = PRE_RK + RET_QK
PRE_WIDTH = PRE_RV + RET_V


def _rms(x, w):
    return x * lax.rsqrt(jnp.mean(x * x, axis=-1, keepdims=True) + NORM_EPS) * w


def _dot(a, b):
    return jnp.dot(a, b, preferred_element_type=F32)


def _dot_nt(a, b):
    return lax.dot_general(a, b, (((1,), (1,)), ((), ())), preferred_element_type=F32)


def _dot_tn(a, b):
    return lax.dot_general(a, b, (((0,), (0,)), ((), ())), preferred_element_type=F32)


def _const_spec(shape):
    nd = len(shape)
    return pl.BlockSpec(shape, lambda *_: (0,) * nd, pipeline_mode=pl.Buffered(1))


def _load_as_bf16(w_hbm, w_ref, stage, sem):
    slots, chunk = stage.shape[0], stage.shape[1]
    n = w_hbm.shape[0] // chunk

    def copy(i):
        return pltpu.make_async_copy(w_hbm.at[pl.ds(i * chunk, chunk), :], stage.at[i % slots],
                                     sem.at[i % slots])

    for i in range(min(slots - 1, n)):
        copy(i).start()
    for i in range(n):
        if i + slots - 1 < n:
            copy(i + slots - 1).start()
        copy(i).wait()
        w_ref[i * chunk:(i + 1) * chunk, :] = stage[i % slots].astype(BF16)


def _ffn_kernel(h_ref, pre_ref, w1_hbm, w2_hbm, post_ref, o_ref, w1_ref, w2_ref, stage1, stage2, sem1, sem2,
                *, parts):
    @pl.when(pl.program_id(0) == 0)
    def _():
        _load_as_bf16(w1_hbm, w1_ref, stage1, sem1)
        _load_as_bf16(w2_hbm, w2_ref, stage2, sem2)

    rows = h_ref.shape[0] // parts
    sl = [slice(p * rows, (p + 1) * rows) for p in range(parts)]
    xn = [_rms(h_ref[s, :], pre_ref[...]).astype(BF16) for s in sl]
    gu = [_dot(x, w1_ref[...]) for x in xn]
    f = []
    for p in range(parts):
        g, u = gu[p][:, :FF_DIM], gu[p][:, FF_DIM:]
        f.append(_dot((g * jax.nn.sigmoid(g) * u).astype(BF16), w2_ref[...]))
    for p, s in enumerate(sl):
        o_ref[s, :] = h_ref[s, :] + 0.5 * _rms(f[p], post_ref[...])


def _ffn(h, pre_w, w1, w2, post_w, tm):
    t = h.shape[0]
    row = pl.BlockSpec((tm, D_MODEL), lambda i: (i, 0))
    return pl.pallas_call(
        functools.partial(_ffn_kernel, parts=FFN_PARTS),
        grid=(t // tm,),
        in_specs=[row, _const_spec((1, D_MODEL)), pl.BlockSpec(memory_space=pl.ANY),
                  pl.BlockSpec(memory_space=pl.ANY), _const_spec((1, D_MODEL))],
        out_specs=row,
        out_shape=jax.ShapeDtypeStruct((t, D_MODEL), F32),
        scratch_shapes=[pltpu.VMEM((D_MODEL, 2 * FF_DIM), BF16), pltpu.VMEM((FF_DIM, D_MODEL), BF16),
                        pltpu.VMEM((FFN_W_SLOTS, FFN_W1_CHUNK, 2 * FF_DIM), F32),
                        pltpu.VMEM((FFN_W_SLOTS, FFN_W2_CHUNK, D_MODEL), F32),
                        pltpu.SemaphoreType.DMA((FFN_W_SLOTS,)), pltpu.SemaphoreType.DMA((FFN_W_SLOTS,))],
        compiler_params=pltpu.CompilerParams(dimension_semantics=("arbitrary",),
                                             vmem_limit_bytes=VMEM_LIMIT),
        name="ffn",
    )(h, pre_w, w1, w2, post_w)


def _rope_lanes(x, c, s):
    return x * c + pltpu.roll(x, LANES - MLA_ROPE, 1) * s


def _tile4(x):
    y = x + pltpu.roll(x, 32, 1)
    return y + pltpu.roll(y, 64, 1)


def _pre_kernel(h_ref, pos_ref, rows_ref, pre_ref, win_ref, qn_ref, wuq_ref, kvn_ref, wuk_ref, wuvt_ref,
                q_ref, k_ref, vt_ref, rq_ref, rk_ref, rv_ref, *, tk):
    parts = h_ref.shape[0] // tk
    sl = [slice(p * tk, (p + 1) * tk) for p in range(parts)]
    rows = rows_ref[...]
    half = RET_QK // 2
    k_scale = RET_DK ** -0.5

    hn = [_rms(h_ref[s, :], pre_ref[...]).astype(BF16) for s in sl]
    lat = [_dot(x, win_ref[:, :PRE_RQ]) for x in hn]

    up = []
    for p in range(parts):
        cq = _rms(lat[p][:, PRE_CQ:PRE_CKV], qn_ref[...]).astype(BF16)
        ckv = _rms(lat[p][:, PRE_CKV:PRE_KR], kvn_ref[...]).astype(BF16)
        up.append((_dot(cq, wuq_ref[...]), _dot(ckv, wuk_ref[...]), _dot_nt(wuvt_ref[...], ckv)))

    ret = [_dot(x, win_ref[:, PRE_RQ:]) for x in hn]

    for p, s in enumerate(sl):
        q_raw, k_raw, vt = up[p]
        ang = pos_ref[p * tk // 2:(p + 1) * tk // 2, :] * (rows[0:1, :] + rows[4:5, :])
        cos, sin = jnp.cos(ang), jnp.sin(ang)
        tabs = []
        for m_mla, m_ret, shift in ((rows[2:3, :], rows[3:4, :], 32), (rows[5:6, :], rows[6:7, :], 96)):
            tabs.append((rows[1:2, :] + pltpu.roll(cos * m_mla, shift, 1),
                         pltpu.roll(sin * (m_mla * rows[7:8, :]), shift, 1),
                         _tile4(cos * m_ret), _tile4(sin * m_ret)))
        c_q, s_q, c_r, s_r = (jnp.concatenate([a, b], axis=0) for a, b in zip(*tabs))

        kr = _rope_lanes(lat[p][:, PRE_KR:PRE_RQ], c_q, s_q)
        for hd in range(MLA_HEADS):
            cols = slice(hd * HEAD_PAD, (hd + 1) * HEAD_PAD)
            q_ref[s, cols] = (_rope_lanes(q_raw[:, cols], c_q, s_q) * ATTN_LOG2_SCALE).astype(BF16)
            k_ref[s, cols] = (k_raw[:, cols] + kr).astype(BF16)
        row = lax.broadcasted_iota(jnp.int32, vt.shape, 0)
        vt = jnp.where(row % HEAD_PAD == MLA_V, 1.0, vt).astype(BF16)
        for hd in range(MLA_HEADS):
            vt_ref[0, hd, p] = vt[hd * HEAD_PAD:(hd + 1) * HEAD_PAD, :]

        rq1, rq2 = ret[p][:, :half], ret[p][:, half:RET_QK]
        rq_ref[s, :half] = (rq1 * c_r - rq2 * s_r).astype(BF16)
        rq_ref[s, half:] = (rq2 * c_r + rq1 * s_r).astype(BF16)
        rk1, rk2 = ret[p][:, RET_QK:RET_QK + half], ret[p][:, RET_QK + half:2 * RET_QK]
        rk_ref[s, :half] = (rk1 * c_r - rk2 * s_r) * k_scale
        rk_ref[s, half:] = (rk2 * c_r + rk1 * s_r) * k_scale
        rv_ref[s, :] = ret[p][:, 2 * RET_QK:].astype(BF16)


def _mixer_pre(h, pos, rows, pre_w, w_in_pre, qn_w, w_uq, kvn_w, w_uk, w_uvt, batch, seq, tm, tk):
    t = h.shape[0]
    ns = seq // tm

    def row(width):
        return pl.BlockSpec((tm, width), lambda i: (i, 0))

    qk_w = MLA_HEADS * HEAD_PAD
    vt_shape = (batch, MLA_HEADS, seq // tk, HEAD_PAD, tk)
    vt_spec = pl.BlockSpec((1, MLA_HEADS, tm // tk, HEAD_PAD, tk), lambda i: (i // ns, 0, i % ns, 0, 0))
    return pl.pallas_call(
        functools.partial(_pre_kernel, tk=tk),
        grid=(t // tm,),
        in_specs=[row(D_MODEL), pl.BlockSpec((tm // 2, LANES), lambda i: (i, 0)),
                  _const_spec((8, LANES)), _const_spec((1, D_MODEL)),
                  _const_spec((D_MODEL, PRE_WIDTH)), _const_spec((1, MLA_Q_RANK)),
                  _const_spec((MLA_Q_RANK, qk_w)), _const_spec((1, MLA_KV_RANK)),
                  _const_spec((MLA_KV_RANK, qk_w)), _const_spec((qk_w, MLA_KV_RANK))],
        out_specs=[row(qk_w), row(qk_w), vt_spec, row(RET_QK), row(RET_QK), row(RET_V)],
        out_shape=[jax.ShapeDtypeStruct((t, qk_w), BF16), jax.ShapeDtypeStruct((t, qk_w), BF16),
                   jax.ShapeDtypeStruct(vt_shape, BF16), jax.ShapeDtypeStruct((t, RET_QK), BF16),
                   jax.ShapeDtypeStruct((t, RET_QK), F32), jax.ShapeDtypeStruct((t, RET_V), BF16)],
        compiler_params=pltpu.CompilerParams(dimension_semantics=("parallel",),
                                             vmem_limit_bytes=VMEM_LIMIT),
        name="mixer_pre",
    )(h, pos, rows, pre_w, w_in_pre, qn_w, w_uq, kvn_w, w_uk, w_uvt)


def _attn_kernel(q_ref, k_ref, vt_ref, o_ref, sa_ref, sb_ref, m_ref, acc_ref, *, tq, kb):
    qi = pl.program_id(2)
    heads = q_ref.shape[1] // HEAD_PAD
    nb = tq // (2 * kb)

    def qk(blk, n, s_ref, row0=0, q_lo=0):
        start = pl.multiple_of(blk * kb, kb)
        tile_max = []
        for hd in range(heads):
            cols = slice(hd * HEAD_PAD, (hd + 1) * HEAD_PAD)
            s = _dot_nt(k_ref[pl.ds(start, n * kb), cols], q_ref[q_lo:, cols])
            s_ref[hd, row0:row0 + n * kb, q_lo:] = s
            tile_max.append(jnp.max(s, axis=0, keepdims=True))
        return tuple(tile_max)

    def softmax_pv(blk, n, s_ref, tile_max, row0=0, q_lo=0, masked=False):
        for hd in range(heads):
            s = s_ref[hd, row0:row0 + n * kb, q_lo:]
            if masked:
                keep = (lax.broadcasted_iota(jnp.int32, s.shape, 0)
                        <= lax.broadcasted_iota(jnp.int32, s.shape, 1))
                s = jnp.where(keep, s, -jnp.inf)
                s_max = jnp.max(s, axis=0, keepdims=True)
            else:
                s_max = tile_max[hd]
            m = m_ref[hd, :, q_lo:]
            m_new = jnp.maximum(m, s_max)
            m_ref[hd, :, q_lo:] = m_new
            p = jnp.exp2(s - m_new).astype(BF16)
            vt = jnp.concatenate([vt_ref[hd, blk + i] for i in range(n)], axis=1)
            acc_ref[hd, :, q_lo:] = (jnp.exp2(m - m_new) * acc_ref[hd, :, q_lo:]
                                     + _dot(vt, p))

    def step(jj, max_a):
        blk = 2 * nb * jj
        max_b = qk(blk + nb, nb, sb_ref)
        softmax_pv(blk, nb, sa_ref, max_a)
        max_a = qk(blk + 2 * nb, nb, sa_ref)
        softmax_pv(blk + nb, nb, sb_ref, max_b)
        return max_a

    m_ref[...] = jnp.full(m_ref.shape, -jnp.inf, F32)
    acc_ref[...] = jnp.zeros(acc_ref.shape, F32)
    lax.fori_loop(0, qi, step, qk(0, nb, sa_ref))
    blk = 2 * nb * qi
    qk(blk + nb, nb, sb_ref, q_lo=nb * kb)
    softmax_pv(blk, nb, sa_ref, None, masked=True)
    softmax_pv(blk + nb, nb, sb_ref, None, q_lo=nb * kb, masked=True)
    o_t = jnp.concatenate([acc_ref[hd, :MLA_V, :] / acc_ref[hd, MLA_V:MLA_V + 1, :] for hd in range(heads)],
                          axis=0)
    o_ref[...] = o_t.T.astype(o_ref.dtype)


def _attention(q, k, vt, batch, seq, tq, kb):
    t = q.shape[0]
    nq = seq // tq
    hg = ATTN_HEADS_PER_STEP
    return pl.pallas_call(
        functools.partial(_attn_kernel, tq=tq, kb=kb),
        grid=(batch, MLA_HEADS // hg, nq),
        in_specs=[pl.BlockSpec((tq, hg * HEAD_PAD), lambda b, p, i: (b * nq + i, p)),
                  pl.BlockSpec((seq, hg * HEAD_PAD), lambda b, p, i: (b, p)),
                  pl.BlockSpec((None, hg, seq // kb, HEAD_PAD, kb), lambda b, p, i: (b, p, 0, 0, 0))],
        out_specs=pl.BlockSpec((tq, hg * MLA_V), lambda b, p, i: (b * nq + i, p)),
        out_shape=jax.ShapeDtypeStruct((t, MLA_HEADS * MLA_V), BF16),
        compiler_params=pltpu.CompilerParams(
            dimension_semantics=("parallel", "parallel", "arbitrary"),
            vmem_limit_bytes=VMEM_LIMIT),
        scratch_shapes=[pltpu.VMEM((hg, tq // 2, tq), F32), pltpu.VMEM((hg, tq // 2, tq), F32),
                        pltpu.VMEM((hg, 1, tq), F32), pltpu.VMEM((hg, HEAD_PAD, tq), F32)],
        name="attention",
    )(q, k, vt)


def _ret_kernel(rq_ref, rk_ref, rv_ref, dec_ref, xi_ref, zeta_ref, cd_ref, mask_ref, y_ref, state_ref,
                *, chunks):
    @pl.when(pl.program_id(1) == 0)
    def _():
        state_ref[...] = jnp.zeros(state_ref.shape, F32)

    blocks = [(c, hd) for c in range(chunks) for hd in range(RET_HEADS)]
    rows = [slice(c * RET_CHUNK, (c + 1) * RET_CHUNK) for c in range(chunks)]
    cols = [slice(hd * RET_DV, (hd + 1) * RET_DV) for hd in range(RET_HEADS)]
    q = [rq_ref[r, :] for r in rows]
    scores, kv = {}, {}
    for c, hd in blocks:
        km = rk_ref[rows[c], :] * mask_ref[hd]
        v = rv_ref[rows[c], cols[hd]]
        scores[c, hd] = _dot_nt(q[c], km.astype(BF16))
        kv[c, hd] = _dot_tn((km * zeta_ref[hd]).astype(BF16), v)
    prev = {}
    for hd in range(RET_HEADS):
        state = state_ref[hd]
        for c in range(chunks):
            prev[c, hd] = state
            state = state * cd_ref[hd] + kv[c, hd]
        state_ref[hd] = state
    for c, hd in blocks:
        inner = (scores[c, hd] * dec_ref[hd]).astype(BF16)
        y = (_dot(inner, rv_ref[rows[c], cols[hd]])
             + _dot(q[c], prev[c, hd].astype(BF16)) * xi_ref[hd])
        mu = jnp.mean(y, axis=-1, keepdims=True)
        yc = y - mu
        var = jnp.mean(yc * yc, axis=-1, keepdims=True)
        y_ref[rows[c], cols[hd]] = yc * lax.rsqrt(var + GN_EPS)


def _retention(rq, rk, rv, consts, batch, seq, tm):
    t = rq.shape[0]
    ns = seq // tm
    dec, xi, zeta, cd, mask = consts

    def row(width):
        return pl.BlockSpec((tm, width), lambda b, i: (b * ns + i, 0))

    return pl.pallas_call(
        functools.partial(_ret_kernel, chunks=tm // RET_CHUNK),
        grid=(batch, ns),
        in_specs=[row(RET_QK), row(RET_QK), row(RET_V), _const_spec(dec.shape), _const_spec(xi.shape),
                  _const_spec(zeta.shape), _const_spec(cd.shape), _const_spec(mask.shape)],
        out_specs=row(RET_V),
        out_shape=jax.ShapeDtypeStruct((t, RET_V), F32),
        scratch_shapes=[pltpu.VMEM((RET_HEADS, RET_QK, RET_DV), F32)],
        compiler_params=pltpu.CompilerParams(dimension_semantics=("parallel", "arbitrary"),
                                             vmem_limit_bytes=VMEM_LIMIT),
        name="retention",
    )(rq, rk, rv, dec, xi, zeta, cd, mask)


def _retention_consts():
    hh = jnp.arange(RET_HEADS, dtype=F32)
    log_gamma = jnp.log(1.0 - 2.0 ** (-5.0 - hh))
    idx = jnp.arange(RET_CHUNK, dtype=F32)
    diff = idx[:, None] - idx[None, :]
    dec = jnp.where(diff >= 0, jnp.exp(jnp.maximum(diff, 0.0) * log_gamma[:, None, None]), 0.0)
    zeta = jnp.exp((RET_CHUNK - 1 - idx) * log_gamma[:, None])
    xi = jnp.exp((idx + 1.0) * log_gamma[:, None])
    cd = jnp.exp(RET_CHUNK * log_gamma)
    xi_b = jnp.broadcast_to(xi[:, :, None], (RET_HEADS, RET_CHUNK, RET_DV))
    zeta_b = jnp.broadcast_to(zeta[:, :, None], (RET_HEADS, RET_CHUNK, RET_QK))
    cd_b = jnp.broadcast_to(cd[:, None, None], (RET_HEADS, 1, RET_DV))
    lane = jnp.arange(RET_QK)
    head_of_lane = (lane % (RET_QK // 2)) // (RET_DK // 2)
    mask = (head_of_lane[None, :] == jnp.arange(RET_HEADS)[:, None]).astype(F32)[:, None, :]
    return dec, xi_b, zeta_b, cd_b, mask


def _post_kernel(h_ref, o_ref, y_ref, pre_ref, wg_ref, gn_ref, wbm_ref, wbr_ref, wout_ref, post_ref,
                 out_ref, *, parts):
    rows = h_ref.shape[0] // parts
    sl = [slice(p * rows, (p + 1) * rows) for p in range(parts)]
    hn = [_rms(h_ref[s, :], pre_ref[...]).astype(BF16) for s in sl]
    rg = [_dot(x, wg_ref[:, :RET_V]) for x in hn]
    o_mla = [_dot(o_ref[s, :], wbm_ref[...]) for s in sl]
    gates = [_dot(x, wg_ref[:, RET_V:]) for x in hn]
    o_ret = []
    for p, s in enumerate(sl):
        a = (rg[p] * jax.nn.sigmoid(rg[p])) * (y_ref[s, :] * gn_ref[...])
        o_ret.append(_dot(a.astype(BF16), wbr_ref[...]))
    m = []
    for p in range(parts):
        merged = (jax.nn.sigmoid(gates[p][:, :D_MODEL]) * o_mla[p]
                  + jax.nn.sigmoid(gates[p][:, D_MODEL:]) * o_ret[p])
        m.append(_dot(merged.astype(BF16), wout_ref[...]))
    for p, s in enumerate(sl):
        out_ref[s, :] = h_ref[s, :] + _rms(m[p], post_ref[...])


def _mixer_post(h, o, y, pre_w, w_gates, gn_w, w_bm, w_br, w_out, post_w, tm):
    t = h.shape[0]

    def row(width):
        return pl.BlockSpec((tm, width), lambda i: (i, 0))

    return pl.pallas_call(
        functools.partial(_post_kernel, parts=POST_PARTS),
        grid=(t // tm,),
        in_specs=[row(D_MODEL), row(MLA_HEADS * MLA_V), row(RET_V), _const_spec((1, D_MODEL)),
                  _const_spec(w_gates.shape), _const_spec((1, RET_V)), _const_spec(w_bm.shape),
                  _const_spec(w_br.shape), _const_spec(w_out.shape), _const_spec((1, D_MODEL))],
        out_specs=row(D_MODEL),
        out_shape=jax.ShapeDtypeStruct((t, D_MODEL), F32),
        compiler_params=pltpu.CompilerParams(dimension_semantics=("parallel",),
                                             vmem_limit_bytes=VMEM_LIMIT),
        name="mixer_post",
    )(h, o, y, pre_w, w_gates, gn_w, w_bm, w_br, w_out, post_w)


def _rope_rows():
    lane = jnp.arange(LANES)
    half_m, half_r = MLA_ROPE // 2, RET_DK // 2
    f_mla = ROPE_BASE ** (-jnp.arange(half_m, dtype=F32) / half_m)
    f_ret = ROPE_BASE ** (-jnp.arange(half_r, dtype=F32) / half_r)
    token = jnp.concatenate([f_ret, f_mla, f_mla, jnp.zeros((LANES // 2,), F32)])
    in_tok = lane % (LANES // 2)
    m_ret = (in_tok < half_r).astype(F32)
    m_mla = (in_tok >= half_r).astype(F32)
    first = (lane < LANES // 2).astype(F32)
    sign = jnp.where(in_tok < half_r, 0.0, jnp.where(in_tok < half_r + half_m, -1.0, 1.0))
    rows = [token, (lane < MLA_NOPE).astype(F32), m_mla * first, m_ret * first,
            jnp.roll(token, LANES // 2), m_mla * (1 - first), m_ret * (1 - first), sign]
    return jnp.stack(rows)


def _split_halves(w, heads, dim):
    h = dim // 2
    return ([w[:, hd * dim:hd * dim + h] for hd in range(heads)]
            + [w[:, hd * dim + h:(hd + 1) * dim] for hd in range(heads)])


def _rope_cols(w):
    h = MLA_ROPE // 2
    return jnp.concatenate([w, w[..., h:], w[..., :h]], axis=-1)


def _layout_w_in(w_in):
    sizes = (MLA_Q_RANK, MLA_KV_RANK, MLA_ROPE, RET_QK, RET_QK, RET_V, RET_V, D_MODEL, D_MODEL)
    parts, off = [], 0
    for s in sizes:
        parts.append(w_in[:, off:off + s])
        off += s
    w_cq, w_ckv, w_kr, w_rq, w_rk, w_rv = parts[:6]
    rows = w_in.shape[0]
    w_pre = jnp.concatenate([w_cq, w_ckv, jnp.zeros((rows, MLA_NOPE), w_in.dtype), _rope_cols(w_kr)]
                            + _split_halves(w_rq, RET_HEADS, RET_DK)
                            + _split_halves(w_rk, RET_HEADS, RET_DK) + [w_rv], axis=1)
    w_gates = w_in[:, w_in.shape[1] - (RET_V + 2 * D_MODEL):]
    return w_pre.astype(BF16), w_gates.astype(BF16)


def _layout_w_uq(w_uq):
    w = w_uq.reshape(MLA_Q_RANK, MLA_HEADS, MLA_NOPE + MLA_ROPE)
    w = jnp.concatenate([w[..., :MLA_NOPE], _rope_cols(w[..., MLA_NOPE:])], axis=-1)
    return w.reshape(MLA_Q_RANK, MLA_HEADS * HEAD_PAD).astype(BF16)


def _layout_w_ukv(w_ukv):
    w = w_ukv.reshape(MLA_KV_RANK, MLA_HEADS, MLA_NOPE + MLA_V)
    w_k = jnp.pad(w[:, :, :MLA_NOPE], ((0, 0), (0, 0), (0, HEAD_PAD - MLA_NOPE)))
    w_v = jnp.pad(w[:, :, MLA_NOPE:], ((0, 0), (0, 0), (0, HEAD_PAD - MLA_V)))
    w_k = w_k.reshape(MLA_KV_RANK, -1).astype(BF16)
    w_vt = w_v.reshape(MLA_KV_RANK, -1).T.astype(BF16)
    return w_k, w_vt


def _pack_positions(positions, tk):
    b, s = positions.shape
    p = positions.reshape(b * s // tk, 2, tk // 2).astype(F32)
    first = p[:, 0, :].reshape(b * s // 2, 1)
    second = p[:, 1, :].reshape(b * s // 2, 1)
    lane = lax.broadcasted_iota(jnp.int32, (b * s // 2, LANES), 1)
    return jnp.where(lane < LANES // 2, first, second)


def _row(w):
    return w.reshape(1, -1)


def kernel(x, positions, ffn1_pre_w, ffn1_w1, ffn1_w2, ffn1_post_w, mix_pre_w, w_in, mla_q_norm_w,
           mla_w_uq, mla_kv_norm_w, mla_w_ukv, ret_gn_w, w_branch_mla, w_branch_ret, w_out, mix_post_w,
           ffn2_pre_w, ffn2_w1, ffn2_w2, ffn2_post_w):
    batch, seq, _ = x.shape
    depth = ffn1_w1.shape[0]
    t = batch * seq
    tm = min(RET_TILE, seq)
    tm_ffn = min(DENSE_TILE, seq)
    tq = min(ATTN_Q_TILE, seq)
    kb = min(ATTN_KEY_BLOCK, seq)
    h = x.reshape(t, D_MODEL)
    pos = _pack_positions(positions, kb)
    rows = _rope_rows()
    ret_consts = _retention_consts()
    for l in range(depth):
        h = _ffn(h, _row(ffn1_pre_w[l]), ffn1_w1[l], ffn1_w2[l],
                 _row(ffn1_post_w[l]), tm_ffn)
        w_pre, w_gates = _layout_w_in(w_in[l])
        w_uk, w_uvt = _layout_w_ukv(mla_w_ukv[l])
        q, k, vt, rq, rk, rv = _mixer_pre(h, pos, rows, _row(mix_pre_w[l]), w_pre, _row(mla_q_norm_w[l]),
                                          _layout_w_uq(mla_w_uq[l]), _row(mla_kv_norm_w[l]), w_uk, w_uvt,
                                          batch, seq, tm_ffn, kb)
        o = _attention(q, k, vt, batch, seq, tq, kb)
        y = _retention(rq, rk, rv, ret_consts, batch, seq, tm)
        h = _mixer_post(h, o, y, _row(mix_pre_w[l]), w_gates, _row(ret_gn_w[l]),
                        w_branch_mla[l].astype(BF16), w_branch_ret[l].astype(BF16), w_out[l].astype(BF16),
                        _row(mix_post_w[l]), tm_ffn)
        h = _ffn(h, _row(ffn2_pre_w[l]), ffn2_w1[l], ffn2_w2[l],
                 _row(ffn2_post_w[l]), tm_ffn)
    return h.reshape(batch, seq, D_MODEL)
```

```python
import functools
import math

import jax
import jax.numpy as jnp
from jax import lax
from jax.experimental import pallas as pl
from jax.experimental.pallas import tpu as pltpu

D_MODEL = 1024
MLA_HEADS = 8
MLA_NOPE = 64
MLA_ROPE = 32
MLA_V = 64
MLA_Q_RANK = 384
MLA_KV_RANK = 256
RET_HEADS = 4
RET_DK = 64
RET_DV = 128
RET_CHUNK = 128
FF_DIM = 2816
ROPE_BASE = 10000.0
NORM_EPS = 1e-6
GN_EPS = 1e-6

LANES = 128
HEAD_PAD = 128
RET_QK = RET_HEADS * RET_DK
RET_V = RET_HEADS * RET_DV
VMEM_LIMIT = 56 * 1024 * 1024
DENSE_TILE = 1024
RET_TILE = 1024
ATTN_Q_TILE = 1024
ATTN_KEY_BLOCK = 256
FFN_PARTS = 4
FFN_W1_CHUNK = 32
FFN_W2_CHUNK = 128
FFN_W_SLOTS = 4
POST_PARTS = 4
ATTN_HEADS_PER_STEP = 4

ATTN_LOG2_SCALE = math.log2(math.e) / math.sqrt(MLA_NOPE + MLA_ROPE)

BF16 = jnp.bfloat16
F32 = jnp.float32

PRE_CQ = 0
PRE_CKV = PRE_CQ + MLA_Q_RANK
PRE_KR = PRE_CKV + MLA_KV_RANK
PRE_RQ = PRE_KR + HEAD_PAD
PRE_RK = PRE_RQ + RET_QK
PRE_RV = PRE_RK + RET_QK
PRE_WIDTH = PRE_RV + RET_V


def _rms(x, w):
    return x * lax.rsqrt(jnp.mean(x * x, axis=-1, keepdims=True) + NORM_EPS) * w


def _dot(a, b):
    return jnp.dot(a, b, preferred_element_type=F32)


def _dot_nt(a, b):
    return lax.dot_general(a, b, (((1,), (1,)), ((), ())), preferred_element_type=F32)


def _dot_tn(a, b):
    return lax.dot_general(a, b, (((0,), (0,)), ((), ())), preferred_element_type=F32)


def _const_spec(shape):
    nd = len(shape)
    return pl.BlockSpec(shape, lambda *_: (0,) * nd, pipeline_mode=pl.Buffered(1))


def _load_as_bf16(w_hbm, w_ref, stage, sem):
    slots, chunk = stage.shape[0], stage.shape[1]
    n = w_hbm.shape[0] // chunk

    def copy(i):
        return pltpu.make_async_copy(w_hbm.at[pl.ds(i * chunk, chunk), :], stage.at[i % slots],
                                     sem.at[i % slots])

    for i in range(min(slots - 1, n)):
        copy(i).start()
    for i in range(n):
        if i + slots - 1 < n:
            copy(i + slots - 1).start()
        copy(i).wait()
        w_ref[i * chunk:(i + 1) * chunk, :] = stage[i % slots].astype(BF16)


def _ffn_kernel(h_ref, pre_ref, w1_hbm, w2_hbm, post_ref, o_ref, w1_ref, w2_ref, stage1, stage2, sem1, sem2,
                *, parts):
    @pl.when(pl.program_id(0) == 0)
    def _():
        _load_as_bf16(w1_hbm, w1_ref, stage1, sem1)
        _load_as_bf16(w2_hbm, w2_ref, stage2, sem2)

    rows = h_ref.shape[0] // parts
    sl = [slice(p * rows, (p + 1) * rows) for p in range(parts)]
    xn = [_rms(h_ref[s, :], pre_ref[...]).astype(BF16) for s in sl]
    gu = [_dot(x, w1_ref[...]) for x in xn]
    f = []
    for p in range(parts):
        g, u = gu[p][:, :FF_DIM], gu[p][:, FF_DIM:]
        f.append(_dot((g * jax.nn.sigmoid(g) * u).astype(BF16), w2_ref[...]))
    for p, s in enumerate(sl):
        o_ref[s, :] = h_ref[s, :] + 0.5 * _rms(f[p], post_ref[...])


def _ffn(h, pre_w, w1, w2, post_w, tm):
    t = h.shape[0]
    row = pl.BlockSpec((tm, D_MODEL), lambda i: (i, 0))
    return pl.pallas_call(
        functools.partial(_ffn_kernel, parts=FFN_PARTS),
        grid=(t // tm,),
        in_specs=[row, _const_spec((1, D_MODEL)), pl.BlockSpec(memory_space=pl.ANY),
                  pl.BlockSpec(memory_space=pl.ANY), _const_spec((1, D_MODEL))],
        out_specs=row,
        out_shape=jax.ShapeDtypeStruct((t, D_MODEL), F32),
        scratch_shapes=[pltpu.VMEM((D_MODEL, 2 * FF_DIM), BF16), pltpu.VMEM((FF_DIM, D_MODEL), BF16),
                        pltpu.VMEM((FFN_W_SLOTS, FFN_W1_CHUNK, 2 * FF_DIM), F32),
                        pltpu.VMEM((FFN_W_SLOTS, FFN_W2_CHUNK, D_MODEL), F32),
                        pltpu.SemaphoreType.DMA((FFN_W_SLOTS,)), pltpu.SemaphoreType.DMA((FFN_W_SLOTS,))],
        compiler_params=pltpu.CompilerParams(dimension_semantics=("arbitrary",),
                                             vmem_limit_bytes=VMEM_LIMIT),
        name="ffn",
    )(h, pre_w, w1, w2, post_w)


def _rope_lanes(x, c, s):
    return x * c + pltpu.roll(x, LANES - MLA_ROPE, 1) * s


def _tile4(x):
    y = x + pltpu.roll(x, 32, 1)
    return y + pltpu.roll(y, 64, 1)


def _pre_kernel(h_ref, pos_ref, rows_ref, pre_ref, win_ref, qn_ref, wuq_ref, kvn_ref, wuk_ref, wuvt_ref,
                q_ref, k_ref, vt_ref, rq_ref, rk_ref, rv_ref, *, tk):
    parts = h_ref.shape[0] // tk
    sl = [slice(p * tk, (p + 1) * tk) for p in range(parts)]
    rows = rows_ref[...]
    half = RET_QK // 2
    k_scale = RET_DK ** -0.5

    hn = [_rms(h_ref[s, :], pre_ref[...]).astype(BF16) for s in sl]
    lat = [_dot(x, win_ref[:, :PRE_RQ]) for x in hn]

    up = []
    for p in range(parts):
        cq = _rms(lat[p][:, PRE_CQ:PRE_CKV], qn_ref[...]).astype(BF16)
        ckv = _rms(lat[p][:, PRE_CKV:PRE_KR], kvn_ref[...]).astype(BF16)
        up.append((_dot(cq, wuq_ref[...]), _dot(ckv, wuk_ref[...]), _dot_nt(wuvt_ref[...], ckv)))

    ret = [_dot(x, win_ref[:, PRE_RQ:]) for x in hn]

    for p, s in enumerate(sl):
        q_raw, k_raw, vt = up[p]
        ang = pos_ref[p * tk // 2:(p + 1) * tk // 2, :] * (rows[0:1, :] + rows[4:5, :])
        cos, sin = jnp.cos(ang), jnp.sin(ang)
        tabs = []
        for m_mla, m_ret, shift in ((rows[2:3, :], rows[3:4, :], 32), (rows[5:6, :], rows[6:7, :], 96)):
            tabs.append((rows[1:2, :] + pltpu.roll(cos * m_mla, shift, 1),
                         pltpu.roll(sin * (m_mla * rows[7:8, :]), shift, 1),
                         _tile4(cos * m_ret), _tile4(sin * m_ret)))
        c_q, s_q, c_r, s_r = (jnp.concatenate([a, b], axis=0) for a, b in zip(*tabs))

        kr = _rope_lanes(lat[p][:, PRE_KR:PRE_RQ], c_q, s_q)
        c_qs, s_qs = c_q * ATTN_LOG2_SCALE, s_q * ATTN_LOG2_SCALE
        for hd in range(MLA_HEADS):
            cols = slice(hd * HEAD_PAD, (hd + 1) * HEAD_PAD)
            q_ref[s, cols] = _rope_lanes(q_raw[:, cols], c_qs, s_qs).astype(BF16)
            k_ref[s, cols] = (k_raw[:, cols] + kr).astype(BF16)
        row = lax.broadcasted_iota(jnp.int32, vt.shape, 0)
        vt = jnp.where(row % HEAD_PAD == MLA_V, 1.0, vt).astype(BF16)
        for hd in range(MLA_HEADS):
            vt_ref[0, hd, p] = vt[hd * HEAD_PAD:(hd + 1) * HEAD_PAD, :]

        rq1, rq2 = ret[p][:, :half], ret[p][:, half:RET_QK]
        rq_ref[s, :half] = (rq1 * c_r - rq2 * s_r).astype(BF16)
        rq_ref[s, half:] = (rq2 * c_r + rq1 * s_r).astype(BF16)
        rk1, rk2 = ret[p][:, RET_QK:RET_QK + half], ret[p][:, RET_QK + half:2 * RET_QK]
        rk_ref[s, :half] = (rk1 * c_r - rk2 * s_r) * k_scale
        rk_ref[s, half:] = (rk2 * c_r + rk1 * s_r) * k_scale
        rv_ref[s, :] = ret[p][:, 2 * RET_QK:].astype(BF16)


def _mixer_pre(h, pos, rows, pre_w, w_in_pre, qn_w, w_uq, kvn_w, w_uk, w_uvt, batch, seq, tm, tk):
    t = h.shape[0]
    ns = seq // tm

    def row(width):
        return pl.BlockSpec((tm, width), lambda i: (i, 0))

    qk_w = MLA_HEADS * HEAD_PAD
    vt_shape = (batch, MLA_HEADS, seq // tk, HEAD_PAD, tk)
    vt_spec = pl.BlockSpec((1, MLA_HEADS, tm // tk, HEAD_PAD, tk), lambda i: (i // ns, 0, i % ns, 0, 0))
    return pl.pallas_call(
        functools.partial(_pre_kernel, tk=tk),
        grid=(t // tm,),
        in_specs=[row(D_MODEL), pl.BlockSpec((tm // 2, LANES), lambda i: (i, 0)),
                  _const_spec((8, LANES)), _const_spec((1, D_MODEL)),
                  _const_spec((D_MODEL, PRE_WIDTH)), _const_spec((1, MLA_Q_RANK)),
                  _const_spec((MLA_Q_RANK, qk_w)), _const_spec((1, MLA_KV_RANK)),
                  _const_spec((MLA_KV_RANK, qk_w)), _const_spec((qk_w, MLA_KV_RANK))],
        out_specs=[row(qk_w), row(qk_w), vt_spec, row(RET_QK), row(RET_QK), row(RET_V)],
        out_shape=[jax.ShapeDtypeStruct((t, qk_w), BF16), jax.ShapeDtypeStruct((t, qk_w), BF16),
                   jax.ShapeDtypeStruct(vt_shape, BF16), jax.ShapeDtypeStruct((t, RET_QK), BF16),
                   jax.ShapeDtypeStruct((t, RET_QK), F32), jax.ShapeDtypeStruct((t, RET_V), BF16)],
        compiler_params=pltpu.CompilerParams(dimension_semantics=("parallel",),
                                             vmem_limit_bytes=VMEM_LIMIT),
        name="mixer_pre",
    )(h, pos, rows, pre_w, w_in_pre, qn_w, w_uq, kvn_w, w_uk, w_uvt)


def _attn_kernel(q_ref, k_ref, vt_ref, o_ref, sa_ref, sb_ref, m_ref, acc_ref, *, tq, kb):
    qi = pl.program_id(2)
    heads = q_ref.shape[1] // HEAD_PAD
    nb = tq // (2 * kb)

    def qk(blk, n, s_ref, row0=0, q_lo=0):
        start = pl.multiple_of(blk * kb, kb)
        tile_max = []
        for hd in range(heads):
            cols = slice(hd * HEAD_PAD, (hd + 1) * HEAD_PAD)
            s = _dot_nt(k_ref[pl.ds(start, n * kb), cols], q_ref[q_lo:, cols])
            s_ref[hd, row0:row0 + n * kb, q_lo:] = s
            tile_max.append(jnp.max(s, axis=0, keepdims=True))
        return tuple(tile_max)

    def softmax_pv(blk, n, s_ref, tile_max, row0=0, q_lo=0, masked=False):
        for hd in range(heads):
            s = s_ref[hd, row0:row0 + n * kb, q_lo:]
            if masked:
                keep = (lax.broadcasted_iota(jnp.int32, s.shape, 0)
                        <= lax.broadcasted_iota(jnp.int32, s.shape, 1))
                s = jnp.where(keep, s, -jnp.inf)
                s_max = jnp.max(s, axis=0, keepdims=True)
            else:
                s_max = tile_max[hd]
            m = m_ref[hd, :, q_lo:]
            m_new = jnp.maximum(m, s_max)
            m_ref[hd, :, q_lo:] = m_new
            p = jnp.exp2(s - m_new).astype(BF16)
            vt = jnp.concatenate([vt_ref[hd, blk + i] for i in range(n)], axis=1)
            acc_ref[hd, :, q_lo:] = (jnp.exp2(m - m_new) * acc_ref[hd, :, q_lo:]
                                     + _dot(vt, p))

    def step(jj, max_a):
        blk = 2 * nb * jj
        max_b = qk(blk + nb, nb, sb_ref)
        softmax_pv(blk, nb, sa_ref, max_a)
        max_a = qk(blk + 2 * nb, nb, sa_ref)
        softmax_pv(blk + nb, nb, sb_ref, max_b)
        return max_a

    m_ref[...] = jnp.full(m_ref.shape, -jnp.inf, F32)
    acc_ref[...] = jnp.zeros(acc_ref.shape, F32)
    lax.fori_loop(0, qi, step, qk(0, nb, sa_ref))
    blk = 2 * nb * qi
    qk(blk + nb, nb, sb_ref, q_lo=nb * kb)
    softmax_pv(blk, nb, sa_ref, None, masked=True)
    softmax_pv(blk + nb, nb, sb_ref, None, q_lo=nb * kb, masked=True)
    o_t = jnp.concatenate([acc_ref[hd, :MLA_V, :] / acc_ref[hd, MLA_V:MLA_V + 1, :] for hd in range(heads)],
                          axis=0)
    o_ref[...] = o_t.T.astype(o_ref.dtype)


def _attention(q, k, vt, batch, seq, tq, kb):
    t = q.shape[0]
    nq = seq // tq
    hg = ATTN_HEADS_PER_STEP
    return pl.pallas_call(
        functools.partial(_attn_kernel, tq=tq, kb=kb),
        grid=(batch, MLA_HEADS // hg, nq),
        in_specs=[pl.BlockSpec((tq, hg * HEAD_PAD), lambda b, p, i: (b * nq + i, p)),
                  pl.BlockSpec((seq, hg * HEAD_PAD), lambda b, p, i: (b, p)),
                  pl.BlockSpec((None, hg, seq // kb, HEAD_PAD, kb), lambda b, p, i: (b, p, 0, 0, 0))],
        out_specs=pl.BlockSpec((tq, hg * MLA_V), lambda b, p, i: (b * nq + i, p)),
        out_shape=jax.ShapeDtypeStruct((t, MLA_HEADS * MLA_V), BF16),
        compiler_params=pltpu.CompilerParams(
            dimension_semantics=("parallel", "parallel", "arbitrary"),
            vmem_limit_bytes=VMEM_LIMIT),
        scratch_shapes=[pltpu.VMEM((hg, tq // 2, tq), F32), pltpu.VMEM((hg, tq // 2, tq), F32),
                        pltpu.VMEM((hg, 1, tq), F32), pltpu.VMEM((hg, HEAD_PAD, tq), F32)],
        name="attention",
    )(q, k, vt)


def _ret_kernel(rq_ref, rk_ref, rv_ref, dec_ref, xi_ref, zeta_ref, cd_ref, mask_ref, y_ref, state_ref,
                *, chunks):
    @pl.when(pl.program_id(1) == 0)
    def _():
        state_ref[...] = jnp.zeros(state_ref.shape, F32)

    blocks = [(c, hd) for c in range(chunks) for hd in range(RET_HEADS)]
    rows = [slice(c * RET_CHUNK, (c + 1) * RET_CHUNK) for c in range(chunks)]
    cols = [slice(hd * RET_DV, (hd + 1) * RET_DV) for hd in range(RET_HEADS)]
    q = [rq_ref[r, :] for r in rows]
    scores, kv = {}, {}
    for c, hd in blocks:
        km = rk_ref[rows[c], :] * mask_ref[hd]
        v = rv_ref[rows[c], cols[hd]]
        scores[c, hd] = _dot_nt(q[c], km.astype(BF16))
        kv[c, hd] = _dot_tn((km * zeta_ref[hd]).astype(BF16), v)
    prev = {}
    for hd in range(RET_HEADS):
        state = state_ref[hd]
        for c in range(chunks):
            prev[c, hd] = state
            state = state * cd_ref[hd] + kv[c, hd]
        state_ref[hd] = state
    for c, hd in blocks:
        inner = (scores[c, hd] * dec_ref[hd]).astype(BF16)
        y = (_dot(inner, rv_ref[rows[c], cols[hd]])
             + _dot(q[c], prev[c, hd].astype(BF16)) * xi_ref[hd])
        mu = jnp.mean(y, axis=-1, keepdims=True)
        yc = y - mu
        var = jnp.mean(yc * yc, axis=-1, keepdims=True)
        y_ref[rows[c], cols[hd]] = yc * lax.rsqrt(var + GN_EPS)


def _retention(rq, rk, rv, consts, batch, seq, tm):
    t = rq.shape[0]
    ns = seq // tm
    dec, xi, zeta, cd, mask = consts

    def row(width):
        return pl.BlockSpec((tm, width), lambda b, i: (b * ns + i, 0))

    return pl.pallas_call(
        functools.partial(_ret_kernel, chunks=tm // RET_CHUNK),
        grid=(batch, ns),
        in_specs=[row(RET_QK), row(RET_QK), row(RET_V), _const_spec(dec.shape), _const_spec(xi.shape),
                  _const_spec(zeta.shape), _const_spec(cd.shape), _const_spec(mask.shape)],
        out_specs=row(RET_V),
        out_shape=jax.ShapeDtypeStruct((t, RET_V), F32),
        scratch_shapes=[pltpu.VMEM((RET_HEADS, RET_QK, RET_DV), F32)],
        compiler_params=pltpu.CompilerParams(dimension_semantics=("parallel", "arbitrary"),
                                             vmem_limit_bytes=VMEM_LIMIT),
        name="retention",
    )(rq, rk, rv, dec, xi, zeta, cd, mask)


def _retention_consts():
    hh = jnp.arange(RET_HEADS, dtype=F32)
    log_gamma = jnp.log(1.0 - 2.0 ** (-5.0 - hh))
    idx = jnp.arange(RET_CHUNK, dtype=F32)
    diff = idx[:, None] - idx[None, :]
    dec = jnp.where(diff >= 0, jnp.exp(jnp.maximum(diff, 0.0) * log_gamma[:, None, None]), 0.0)
    zeta = jnp.exp((RET_CHUNK - 1 - idx) * log_gamma[:, None])
    xi = jnp.exp((idx + 1.0) * log_gamma[:, None])
    cd = jnp.exp(RET_CHUNK * log_gamma)
    xi_b = jnp.broadcast_to(xi[:, :, None], (RET_HEADS, RET_CHUNK, RET_DV))
    zeta_b = jnp.broadcast_to(zeta[:, :, None], (RET_HEADS, RET_CHUNK, RET_QK))
    cd_b = jnp.broadcast_to(cd[:, None, None], (RET_HEADS, 1, RET_DV))
    lane = jnp.arange(RET_QK)
    head_of_lane = (lane % (RET_QK // 2)) // (RET_DK // 2)
    mask = (head_of_lane[None, :] == jnp.arange(RET_HEADS)[:, None]).astype(F32)[:, None, :]
    return dec, xi_b, zeta_b, cd_b, mask


def _post_kernel(h_ref, o_ref, y_ref, pre_ref, wg_ref, gn_ref, wbm_ref, wbr_ref, wout_ref, post_ref,
                 out_ref, *, parts):
    rows = h_ref.shape[0] // parts
    sl = [slice(p * rows, (p + 1) * rows) for p in range(parts)]
    hn = [_rms(h_ref[s, :], pre_ref[...]).astype(BF16) for s in sl]
    rg = [_dot(x, wg_ref[:, :RET_V]) for x in hn]
    o_mla = [_dot(o_ref[s, :], wbm_ref[...]) for s in sl]
    gates = [_dot(x, wg_ref[:, RET_V:]) for x in hn]
    o_ret = []
    for p, s in enumerate(sl):
        a = (rg[p] * jax.nn.sigmoid(rg[p])) * (y_ref[s, :] * gn_ref[...])
        o_ret.append(_dot(a.astype(BF16), wbr_ref[...]))
    m = []
    for p in range(parts):
        merged = (jax.nn.sigmoid(gates[p][:, :D_MODEL]) * o_mla[p]
                  + jax.nn.sigmoid(gates[p][:, D_MODEL:]) * o_ret[p])
        m.append(_dot(merged.astype(BF16), wout_ref[...]))
    for p, s in enumerate(sl):
        out_ref[s, :] = h_ref[s, :] + _rms(m[p], post_ref[...])


def _mixer_post(h, o, y, pre_w, w_gates, gn_w, w_bm, w_br, w_out, post_w, tm):
    t = h.shape[0]

    def row(width):
        return pl.BlockSpec((tm, width), lambda i: (i, 0))

    return pl.pallas_call(
        functools.partial(_post_kernel, parts=POST_PARTS),
        grid=(t // tm,),
        in_specs=[row(D_MODEL), row(MLA_HEADS * MLA_V), row(RET_V), _const_spec((1, D_MODEL)),
                  _const_spec(w_gates.shape), _const_spec((1, RET_V)), _const_spec(w_bm.shape),
                  _const_spec(w_br.shape), _const_spec(w_out.shape), _const_spec((1, D_MODEL))],
        out_specs=row(D_MODEL),
        out_shape=jax.ShapeDtypeStruct((t, D_MODEL), F32),
        compiler_params=pltpu.CompilerParams(dimension_semantics=("parallel",),
                                             vmem_limit_bytes=VMEM_LIMIT),
        name="mixer_post",
    )(h, o, y, pre_w, w_gates, gn_w, w_bm, w_br, w_out, post_w)


def _rope_rows():
    lane = jnp.arange(LANES)
    half_m, half_r = MLA_ROPE // 2, RET_DK // 2
    f_mla = ROPE_BASE ** (-jnp.arange(half_m, dtype=F32) / half_m)
    f_ret = ROPE_BASE ** (-jnp.arange(half_r, dtype=F32) / half_r)
    token = jnp.concatenate([f_ret, f_mla, f_mla, jnp.zeros((LANES // 2,), F32)])
    in_tok = lane % (LANES // 2)
    m_ret = (in_tok < half_r).astype(F32)
    m_mla = (in_tok >= half_r).astype(F32)
    first = (lane < LANES // 2).astype(F32)
    sign = jnp.where(in_tok < half_r, 0.0, jnp.where(in_tok < half_r + half_m, -1.0, 1.0))
    rows = [token, (lane < MLA_NOPE).astype(F32), m_mla * first, m_ret * first,
            jnp.roll(token, LANES // 2), m_mla * (1 - first), m_ret * (1 - first), sign]
    return jnp.stack(rows)


def _split_halves(w, heads, dim):
    h = dim // 2
    return ([w[:, hd * dim:hd * dim + h] for hd in range(heads)]
            + [w[:, hd * dim + h:(hd + 1) * dim] for hd in range(heads)])


def _rope_cols(w):
    h = MLA_ROPE // 2
    return jnp.concatenate([w, w[..., h:], w[..., :h]], axis=-1)


def _layout_w_in(w_in):
    sizes = (MLA_Q_RANK, MLA_KV_RANK, MLA_ROPE, RET_QK, RET_QK, RET_V, RET_V, D_MODEL, D_MODEL)
    parts, off = [], 0
    for s in sizes:
        parts.append(w_in[:, off:off + s])
        off += s
    w_cq, w_ckv, w_kr, w_rq, w_rk, w_rv = parts[:6]
    rows = w_in.shape[0]
    w_pre = jnp.concatenate([w_cq, w_ckv, jnp.zeros((rows, MLA_NOPE), w_in.dtype), _rope_cols(w_kr)]
                            + _split_halves(w_rq, RET_HEADS, RET_DK)
                            + _split_halves(w_rk, RET_HEADS, RET_DK) + [w_rv], axis=1)
    w_gates = w_in[:, w_in.shape[1] - (RET_V + 2 * D_MODEL):]
    return w_pre.astype(BF16), w_gates.astype(BF16)


def _layout_w_uq(w_uq):
    w = w_uq.reshape(MLA_Q_RANK, MLA_HEADS, MLA_NOPE + MLA_ROPE)
    w = jnp.concatenate([w[..., :MLA_NOPE], _rope_cols(w[..., MLA_NOPE:])], axis=-1)
    return w.reshape(MLA_Q_RANK, MLA_HEADS * HEAD_PAD).astype(BF16)


def _layout_w_ukv(w_ukv):
    w = w_ukv.reshape(MLA_KV_RANK, MLA_HEADS, MLA_NOPE + MLA_V)
    w_k = jnp.pad(w[:, :, :MLA_NOPE], ((0, 0), (0, 0), (0, HEAD_PAD - MLA_NOPE)))
    w_v = jnp.pad(w[:, :, MLA_NOPE:], ((0, 0), (0, 0), (0, HEAD_PAD - MLA_V)))
    w_k = w_k.reshape(MLA_KV_RANK, -1).astype(BF16)
    w_vt = w_v.reshape(MLA_KV_RANK, -1).T.astype(BF16)
    return w_k, w_vt


def _pack_positions(positions, tk):
    b, s = positions.shape
    p = positions.reshape(b * s // tk, 2, tk // 2).astype(F32)
    first = p[:, 0, :].reshape(b * s // 2, 1)
    second = p[:, 1, :].reshape(b * s // 2, 1)
    lane = lax.broadcasted_iota(jnp.int32, (b * s // 2, LANES), 1)
    return jnp.where(lane < LANES // 2, first, second)


def _row(w):
    return w.reshape(1, -1)


def kernel(x, positions, ffn1_pre_w, ffn1_w1, ffn1_w2, ffn1_post_w, mix_pre_w, w_in, mla_q_norm_w,
           mla_w_uq, mla_kv_norm_w, mla_w_ukv, ret_gn_w, w_branch_mla, w_branch_ret, w_out, mix_post_w,
           ffn2_pre_w, ffn2_w1, ffn2_w2, ffn2_post_w):
    batch, seq, _ = x.shape
    depth = ffn1_w1.shape[0]
    t = batch * seq
    tm = min(RET_TILE, seq)
    tm_dense = min(DENSE_TILE, seq)
    tq = min(ATTN_Q_TILE, seq)
    kb = min(ATTN_KEY_BLOCK, seq)
    h = x.reshape(t, D_MODEL)
    pos = _pack_positions(positions, kb)
    rows = _rope_rows()
    ret_consts = _retention_consts()
    for l in range(depth):
        h = _ffn(h, _row(ffn1_pre_w[l]), ffn1_w1[l], ffn1_w2[l],
                 _row(ffn1_post_w[l]), tm_dense)
        w_pre, w_gates = _layout_w_in(w_in[l])
        w_uk, w_uvt = _layout_w_ukv(mla_w_ukv[l])
        q, k, vt, rq, rk, rv = _mixer_pre(h, pos, rows, _row(mix_pre_w[l]), w_pre, _row(mla_q_norm_w[l]),
                                          _layout_w_uq(mla_w_uq[l]), _row(mla_kv_norm_w[l]), w_uk, w_uvt,
                                          batch, seq, tm_dense, kb)
        o = _attention(q, k, vt, batch, seq, tq, kb)
        y = _retention(rq, rk, rv, ret_consts, batch, seq, tm)
        h = _mixer_post(h, o, y, _row(mix_pre_w[l]), w_gates, _row(ret_gn_w[l]),
                        w_branch_mla[l].astype(BF16), w_branch_ret[l].astype(BF16), w_out[l].astype(BF16),
                        _row(mix_post_w[l]), tm_dense)
        h = _ffn(h, _row(ffn2_pre_w[l]), ffn2_w1[l], ffn2_w2[l],
                 _row(ffn2_post_w[l]), tm_dense)
    return h.reshape(batch, seq, D_MODEL)
```
